```python
import math
import jax, jax.numpy as jnp
from jax import lax
import numpy as np

D_MODEL = 2048
BATCH = 4
SEQ = 4096
DEPTH = 4

D_MIX = D_MODEL
D_S5 = D_MIX // 2
D_RET = D_MIX - D_S5
S5_GROUP = 16
S5_GROUPS = D_S5 // S5_GROUP
S5_STATE = 64
RET_HEAD_DIM = 128
RET_HEADS = D_RET // RET_HEAD_DIM
RET_CHUNK = 128
ROPE_BASE = 10000.0
D_FF = 5632
N_EXPERTS = 8
TOP_K = 2
D_EXPERT = 7168
D_IN = D_S5 + 4 * D_RET
NORM_EPS = 1e-6
GN_EPS = 1e-5

kernel_name = "hymba_s5_retnet_moe_trunk"


def rmsnorm(x, gain):
    x32 = x.astype(jnp.float32)
    y = x32 * lax.rsqrt(jnp.mean(x32 * x32, axis=-1, keepdims=True) + NORM_EPS)
    return (y * gain.astype(jnp.float32)).astype(x.dtype)


def swiglu(h, w_gate, w_up, w_down):
    return (jax.nn.silu(h @ w_gate) * (h @ w_up)) @ w_down


def _cmul_scan_op(e1, e2):
    a1r, a1i, b1r, b1i = e1
    a2r, a2i, b2r, b2i = e2
    ar = a1r * a2r - a1i * a2i
    ai = a1r * a2i + a1i * a2r
    br = a2r * b1r - a2i * b1i + b2r
    bi = a2r * b1i + a2i * b1r + b2i
    return (ar, ai, br, bi)


def s5_mixer(u, lam_re, lam_im, log_dt, b_re, b_im, c_re, c_im, d_skip, w_glu, gain):
    out_dtype = u.dtype
    f32 = jnp.float32
    bsz, seq, _ = u.shape
    ug = u.astype(f32).reshape(bsz, seq, S5_GROUPS, S5_GROUP)
    lr = jnp.minimum(lam_re.astype(f32), -1e-4)
    li = lam_im.astype(f32)
    dt = jnp.exp(log_dt.astype(f32))[:, None]
    mag = jnp.exp(lr * dt)
    ab_re = mag * jnp.cos(li * dt)
    ab_im = mag * jnp.sin(li * dt)
    num_re = ab_re - 1.0
    den = lr * lr + li * li
    f_re = (num_re * lr + ab_im * li) / den
    f_im = (ab_im * lr - num_re * li) / den
    br = b_re.astype(f32)
    bi = b_im.astype(f32)
    bbar_re = f_re[..., None] * br - f_im[..., None] * bi
    bbar_im = f_re[..., None] * bi + f_im[..., None] * br
    bu_re = jnp.einsum('blgh,gph->lbgp', ug, bbar_re)
    bu_im = jnp.einsum('blgh,gph->lbgp', ug, bbar_im)
    a_re = jnp.broadcast_to(ab_re[None, None], (seq, 1, S5_GROUPS, S5_STATE))
    a_im = jnp.broadcast_to(ab_im[None, None], (seq, 1, S5_GROUPS, S5_STATE))
    _, _, xr, xi = lax.associative_scan(_cmul_scan_op, (a_re, a_im, bu_re, bu_im), axis=0)
    y = (jnp.einsum('lbgp,ghp->blgh', xr, c_re.astype(f32))
         - jnp.einsum('lbgp,ghp->blgh', xi, c_im.astype(f32))
         + d_skip.astype(f32) * ug)
    yg = jax.nn.gelu(y)
    y = yg * jax.nn.sigmoid(jnp.einsum('blgh,ghk->blgk', yg, w_glu.astype(f32)))
    y = y.reshape(bsz, seq, D_S5).astype(out_dtype)
    return rmsnorm(y, gain)


def apply_rotary(t, cos, sin):
    half = t.shape[-1] // 2
    t1, t2 = t[..., :half], t[..., half:]
    c = cos[None, :, None, :]
    s = sin[None, :, None, :]
    return jnp.concatenate([t1 * c - t2 * s, t1 * s + t2 * c], axis=-1)


def retention_mixer(q, k, v, g, gain):
    out_dtype = q.dtype
    f32 = jnp.float32
    bsz, seq, _ = q.shape
    nc = seq // RET_CHUNK
    shp = (bsz, seq, RET_HEADS, RET_HEAD_DIM)
    q = q.astype(f32).reshape(shp)
    k = k.astype(f32).reshape(shp) * (RET_HEAD_DIM ** -0.5)
    v = v.astype(f32).reshape(shp)
    pos = jnp.arange(seq, dtype=f32)
    inv_freq = ROPE_BASE ** (-jnp.arange(0, RET_HEAD_DIM, 2, dtype=f32) / RET_HEAD_DIM)
    ang = pos[:, None] * inv_freq[None, :]
    cos, sin = jnp.cos(ang), jnp.sin(ang)
    q = apply_rotary(q, cos, sin)
    k = apply_rotary(k, cos, sin)
    log_g = jnp.log(1.0 - 2.0 ** (-5.0 - jnp.arange(RET_HEADS, dtype=f32)))
    cshp = (bsz, nc, RET_CHUNK, RET_HEADS, RET_HEAD_DIM)
    qc, kc, vc = q.reshape(cshp), k.reshape(cshp), v.reshape(cshp)
    idx = jnp.arange(RET_CHUNK, dtype=f32)
    diff = idx[:, None] - idx[None, :]
    causal = diff >= 0
    mask = jnp.where(causal[None], jnp.exp(jnp.where(causal, diff, 0.0)[None] * log_g[:, None, None]), 0.0)
    scores = jnp.einsum('bcnhd,bcmhd->bchnm', qc, kc) * mask[None, None]
    inner = jnp.einsum('bchnm,bcmhe->bcnhe', scores, vc)
    k_decay = jnp.exp((RET_CHUNK - 1.0 - idx)[:, None] * log_g[None, :])
    q_decay = jnp.exp((idx + 1.0)[:, None] * log_g[None, :])
    kv = jnp.einsum('bcmhd,bcmhe->bchde', kc * k_decay[None, None, :, :, None], vc)
    chunk_decay = jnp.exp(RET_CHUNK * log_g)[None, :, None, None]

    def step(state, kv_c):
        return chunk_decay * state + kv_c, state

    _, s_prev = lax.scan(step, jnp.zeros_like(kv[:, 0]), jnp.moveaxis(kv, 1, 0))
    s_prev = jnp.moveaxis(s_prev, 0, 1)
    cross = jnp.einsum('bcnhd,bchde->bcnhe', qc * q_decay[None, None, :, :, None], s_prev)
    o = (inner + cross).reshape(bsz, seq, RET_HEADS, RET_HEAD_DIM)
    mu = jnp.mean(o, axis=-1, keepdims=True)
    var = jnp.mean(jnp.square(o - mu), axis=-1, keepdims=True)
    o = ((o - mu) * lax.rsqrt(var + GN_EPS)).reshape(bsz, seq, D_RET) * gain.astype(f32)
    return (jax.nn.silu(g.astype(f32)) * o).astype(out_dtype)


def moe_swiglu(h, router_w, router_b, w_gate, w_up, w_down):
    bsz, seq, d = h.shape
    t = h.reshape(-1, d)
    logits = t.astype(jnp.float32) @ router_w.astype(jnp.float32) + router_b.astype(jnp.float32)
    top_vals, top_idx = lax.top_k(logits, TOP_K)
    top_w = jax.nn.softmax(top_vals, axis=-1)
    gates = jnp.sum(jax.nn.one_hot(top_idx, N_EXPERTS, dtype=jnp.float32) * top_w[..., None], axis=-2)
    gates = gates.astype(h.dtype)
    out = jnp.zeros_like(t)
    for e in range(N_EXPERTS):
        out = out + gates[:, e:e + 1] * swiglu(t, w_gate[e], w_up[e], w_down[e])
    return out.reshape(bsz, seq, d)


def setup_inputs(seed: int = 0) -> dict:
    key = jax.random.key(seed)
    ks = iter(jax.random.split(key, 32))
    f32 = jnp.float32
    n_dense = (DEPTH + 1) // 2
    n_moe = DEPTH // 2
    nrm = lambda k, shape, scale: jax.random.normal(k, shape, f32) * scale
    x = jax.random.normal(next(ks), (BATCH, SEQ, D_MODEL), f32)
    ln_mix = 1.0 + nrm(next(ks), (DEPTH, D_MODEL), 0.01)
    ln_ffn = 1.0 + nrm(next(ks), (DEPTH, D_MODEL), 0.01)
    w_in = nrm(next(ks), (DEPTH, D_MODEL, D_IN), D_MODEL ** -0.5)
    w_out = nrm(next(ks), (DEPTH, D_MIX, D_MODEL), D_MIX ** -0.5)
    n_idx = jnp.arange(S5_STATE, dtype=f32)
    s5_lam_re = -0.5 + nrm(next(ks), (DEPTH, S5_GROUPS, S5_STATE), 0.005)
    s5_lam_im = math.pi * n_idx[None, None, :] + nrm(next(ks), (DEPTH, S5_GROUPS, S5_STATE), 0.005)
    s5_log_dt = jax.random.uniform(next(ks), (DEPTH, S5_GROUPS), f32,
                                   minval=math.log(1e-3), maxval=math.log(1e-1))
    s5_b_re = nrm(next(ks), (DEPTH, S5_GROUPS, S5_STATE, S5_GROUP), (2 * S5_GROUP) ** -0.5)
    s5_b_im = nrm(next(ks), (DEPTH, S5_GROUPS, S5_STATE, S5_GROUP), (2 * S5_GROUP) ** -0.5)
    s5_c_re = nrm(next(ks), (DEPTH, S5_GROUPS, S5_GROUP, S5_STATE), (2 * S5_STATE) ** -0.5)
    s5_c_im = nrm(next(ks), (DEPTH, S5_GROUPS, S5_GROUP, S5_STATE), (2 * S5_STATE) ** -0.5)
    s5_d = nrm(next(ks), (DEPTH, S5_GROUPS, S5_GROUP), 1.0)
    s5_w_glu = nrm(next(ks), (DEPTH, S5_GROUPS, S5_GROUP, S5_GROUP), S5_GROUP ** -0.5)
    s5_gain = 1.0 + nrm(next(ks), (DEPTH, D_S5), 0.01)
    ret_gain = 1.0 + nrm(next(ks), (DEPTH, D_RET), 0.01)
    ffn_w_gate = nrm(next(ks), (n_dense, D_MODEL, D_FF), D_MODEL ** -0.5)
    ffn_w_up = nrm(next(ks), (n_dense, D_MODEL, D_FF), D_MODEL ** -0.5)
    ffn_w_down = nrm(next(ks), (n_dense, D_FF, D_MODEL), D_FF ** -0.5)
    router_w = nrm(next(ks), (n_moe, D_MODEL, N_EXPERTS), D_MODEL ** -0.5)
    router_b = nrm(next(ks), (n_moe, N_EXPERTS), 0.01)
    moe_w_gate = nrm(next(ks), (n_moe, N_EXPERTS, D_MODEL, D_EXPERT), D_MODEL ** -0.5)
    moe_w_up = nrm(next(ks), (n_moe, N_EXPERTS, D_MODEL, D_EXPERT), D_MODEL ** -0.5)
    moe_w_down = nrm(next(ks), (n_moe, N_EXPERTS, D_EXPERT, D_MODEL), D_EXPERT ** -0.5)
    final_gain = 1.0 + nrm(next(ks), (D_MODEL,), 0.01)
    return {"x": x, "ln_mix": ln_mix, "ln_ffn": ln_ffn, "w_in": w_in, "w_out": w_out,
            "s5_lam_re": s5_lam_re, "s5_lam_im": s5_lam_im, "s5_log_dt": s5_log_dt,
            "s5_b_re": s5_b_re, "s5_b_im": s5_b_im, "s5_c_re": s5_c_re, "s5_c_im": s5_c_im,
            "s5_d": s5_d, "s5_w_glu": s5_w_glu, "s5_gain": s5_gain, "ret_gain": ret_gain,
            "ffn_w_gate": ffn_w_gate, "ffn_w_up": ffn_w_up, "ffn_w_down": ffn_w_down,
            "router_w": router_w, "router_b": router_b, "moe_w_gate": moe_w_gate,
            "moe_w_up": moe_w_up, "moe_w_down": moe_w_down, "final_gain": final_gain}


def reference(x, ln_mix, ln_ffn, w_in, w_out, s5_lam_re, s5_lam_im, s5_log_dt,
              s5_b_re, s5_b_im, s5_c_re, s5_c_im, s5_d, s5_w_glu, s5_gain, ret_gain,
              ffn_w_gate, ffn_w_up, ffn_w_down, router_w, router_b, moe_w_gate,
              moe_w_up, moe_w_down, final_gain):
    splits = [D_S5, D_S5 + D_RET, D_S5 + 2 * D_RET, D_S5 + 3 * D_RET]
    for i in range(DEPTH):
        h = rmsnorm(x, ln_mix[i])
        proj = h @ w_in[i]
        u, q, k, v, g = jnp.split(proj, splits, axis=-1)
        y_s5 = s5_mixer(u, s5_lam_re[i], s5_lam_im[i], s5_log_dt[i], s5_b_re[i], s5_b_im[i],
                        s5_c_re[i], s5_c_im[i], s5_d[i], s5_w_glu[i], s5_gain[i])
        y_ret = retention_mixer(q, k, v, g, ret_gain[i])
        x = x + jnp.concatenate([y_s5, y_ret], axis=-1) @ w_out[i]
        h = rmsnorm(x, ln_ffn[i])
        j = i // 2
        if i % 2 == 0:
            x = x + swiglu(h, ffn_w_gate[j], ffn_w_up[j], ffn_w_down[j])
        else:
            x = x + moe_swiglu(h, router_w[j], router_b[j], moe_w_gate[j], moe_w_up[j], moe_w_down[j])
    return rmsnorm(x, final_gain)
```

```python
import functools
import math

import jax
import jax.numpy as jnp
from jax import lax
from jax.experimental import pallas as pl
from jax.experimental.pallas import tpu as pltpu

F32 = jnp.float32
BF16 = jnp.bfloat16
HIGHEST = lax.Precision.HIGHEST

S5_GROUP = 16
S5_STATE = 64
S5_LAM_RE_MAX = -1e-4
RET_HEAD_DIM = 128
ROPE_BASE = 10000.0
TOP_K = 2
NORM_EPS = 1e-6
GN_EPS = 1e-5

VMEM_LIMIT_BYTES = 56 * 1024 * 1024

S5_CHUNK = 64
RET_CHUNK = 256
ROW_TILE = 1024
COL_TILE = 1024
FFN_ROW_TILE = 512
FFN_HID_TILE = 512
GATHER_WINDOW = 256


def _params(*semantics):
    return pltpu.CompilerParams(dimension_semantics=semantics,
                                vmem_limit_bytes=VMEM_LIMIT_BYTES)


def _rms_scale(x):
    return x * lax.rsqrt(jnp.mean(x * x, axis=-1, keepdims=True) + NORM_EPS)


def _norm_matmul_kernel(x_ref, g_ref, w_ref, o_ref, h_ref):
    @pl.when(pl.program_id(1) == 0)
    def _():
        h_ref[...] = (_rms_scale(x_ref[...]) * g_ref[...]).astype(BF16)

    o_ref[...] = jnp.dot(h_ref[...], w_ref[...],
                         preferred_element_type=F32).astype(o_ref.dtype)


def norm_matmul(x, gain, w, out_dtype):
    m, k = x.shape
    n = w.shape[1]
    tm, tn = min(ROW_TILE, m), min(COL_TILE, n)
    return pl.pallas_call(
        _norm_matmul_kernel,
        grid=(m // tm, n // tn),
        in_specs=[pl.BlockSpec((tm, k), lambda i, j: (i, 0)),
                  pl.BlockSpec((1, k), lambda i, j: (0, 0)),
                  pl.BlockSpec((k, tn), lambda i, j: (0, j))],
        out_specs=pl.BlockSpec((tm, tn), lambda i, j: (i, j)),
        out_shape=jax.ShapeDtypeStruct((m, n), out_dtype),
        scratch_shapes=[pltpu.VMEM((tm, k), BF16)],
        compiler_params=_params("parallel", "arbitrary"),
        name="norm_matmul",
    )(x, gain.reshape(1, k), w)


def _s5_kernel(u_ref, lrc_ref, lic_ref, lrr_ref, lir_ref, ldt_ref, cx_ref, cy_ref,
               btr_ref, bti_ref, dsk_ref, wglu_ref, z_ref,
               mt_ref, s_ref, xprev_ref, *, batch):
    rows, th = u_ref.shape
    hh = S5_GROUP
    t_len = th // hh
    p2 = 2 * S5_STATE
    n_chunks = rows // batch
    shift = hh.bit_length() - 1

    dt = jnp.exp(ldt_ref[...])

    lrc = jnp.minimum(lrc_ref[...], S5_LAM_RE_MAX) * dt
    lic = lic_ref[...] * dt
    jv = lax.broadcasted_iota(jnp.int32, (p2, t_len), 1).astype(F32)

    def pow_cols(e):
        mag = jnp.exp(lrc * e)
        ang = lic * e
        return mag * jnp.cos(ang), mag * jnp.sin(ang)

    ar0, ai0 = pow_cols(jv)
    ar1, ai1 = pow_cols(jv + 1.0)

    lane_th = lax.broadcasted_iota(jnp.int32, (t_len, th), 1)
    rep = (jnp.right_shift(lane_th, shift)
           == lax.broadcasted_iota(jnp.int32, (t_len, th), 0)).astype(F32)
    til = (jnp.bitwise_and(lax.broadcasted_iota(jnp.int32, (hh, th), 1), hh - 1)
           == lax.broadcasted_iota(jnp.int32, (hh, th), 0)).astype(F32)

    row_p = lax.broadcasted_iota(jnp.int32, (p2, 1), 0)
    top = row_p < S5_STATE
    cx = jnp.dot(cx_ref[...], til, precision=HIGHEST, preferred_element_type=F32)
    cy = jnp.dot(cy_ref[...], til, precision=HIGHEST, preferred_element_type=F32)
    cy = jnp.where(top, -cy, cy)

    def expand_cols(a):
        return jnp.dot(a, rep, precision=HIGHEST, preferred_element_type=F32)

    ca0 = expand_cols(ar0) * cx + expand_cols(ai0) * cy
    ca1 = expand_cols(ar1) * cx + expand_cols(ai1) * cy
    sout_t = jnp.where(top, ca1, -ca1).astype(BF16)

    lrr = jnp.minimum(lrr_ref[...], S5_LAM_RE_MAX)
    lir = lir_ref[...]
    mag = jnp.exp(lrr * dt)
    ab_re = mag * jnp.cos(lir * dt)
    ab_im = mag * jnp.sin(lir * dt)
    num_re = ab_re - 1.0
    den = lrr * lrr + lir * lir
    f_re = (num_re * lrr + ab_im * lir) / den
    f_im = (ab_im * lrr - num_re * lir) / den
    b_re = btr_ref[...]
    b_im = bti_ref[...]
    bb_re = f_re * b_re - f_im * b_im
    bb_im = f_re * b_im + f_im * b_re
    left = lax.broadcasted_iota(jnp.int32, (1, p2), 1) < S5_STATE

    kt = jnp.dot(jnp.where(left, bb_re, -bb_im), ca0,
                 precision=HIGHEST, preferred_element_type=F32)

    ktpad = jnp.concatenate([jnp.zeros_like(kt), kt], axis=1)
    per_vreg = 128 // hh
    rolled = [ktpad] + [pltpu.roll(ktpad, hh * r, axis=1) for r in range(1, per_vreg)]
    for s in range(t_len):
        q, r = divmod(s, per_vreg)
        mt_ref[pl.ds(s * hh, hh), :] = rolled[r][:, th - 128 * q: 2 * th - 128 * q].astype(BF16)

    sv = lax.broadcasted_iota(jnp.int32, (t_len, p2), 0).astype(F32)
    e_in = (t_len - 1.0) - sv
    mag_s = jnp.exp(lrr * dt * e_in)
    ang_s = lir * dt * e_in
    ars = mag_s * jnp.cos(ang_s)
    ais = mag_s * jnp.sin(ang_s)
    row_th = lax.broadcasted_iota(jnp.int32, (th, t_len), 0)
    rep_t = (jnp.right_shift(row_th, shift)
             == lax.broadcasted_iota(jnp.int32, (th, t_len), 1)).astype(F32)
    til_t = (jnp.bitwise_and(lax.broadcasted_iota(jnp.int32, (th, hh), 0), hh - 1)
             == lax.broadcasted_iota(jnp.int32, (th, hh), 1)).astype(F32)
    a_exp = jnp.dot(rep_t, jnp.concatenate([ars, ais], axis=1),
                    precision=HIGHEST, preferred_element_type=F32)
    b_sel = jnp.concatenate([jnp.where(left, bb_re, bb_im), jnp.where(left, -bb_im, bb_re),
                             jnp.where(left, bb_im, bb_re), jnp.where(left, bb_re, -bb_im)],
                            axis=1)
    b_exp = jnp.dot(til_t, b_sel, precision=HIGHEST, preferred_element_type=F32)
    ars_e, ais_e = a_exp[:, :p2], a_exp[:, p2:]
    sin_p = ars_e * b_exp[:, 0 * p2:1 * p2] + ais_e * b_exp[:, 1 * p2:2 * p2]
    sin_q = ars_e * b_exp[:, 2 * p2:3 * p2] + ais_e * b_exp[:, 3 * p2:4 * p2]
    sin_pq = jnp.concatenate([sin_p, sin_q], axis=1).astype(BF16)

    x = u_ref[...]
    y = jnp.dot(x, mt_ref[...], preferred_element_type=F32)
    s_ref[...] = jnp.dot(x, sin_pq, preferred_element_type=F32)

    at_mag = jnp.exp(lrr * dt * float(t_len))
    at_re = at_mag * jnp.cos(lir * dt * float(t_len))
    at_im = at_mag * jnp.sin(lir * dt * float(t_len))
    at_im_p = jnp.where(left, -at_im, at_im)
    st_p = jnp.zeros((batch, p2), F32)
    st_q = jnp.zeros((batch, p2), F32)
    for c in range(n_chunks):
        xprev_ref[pl.ds(c * batch, batch), :] = st_p
        inj_p = s_ref[pl.ds(c * batch, batch), 0:p2]
        inj_q = s_ref[pl.ds(c * batch, batch), p2:2 * p2]
        st_p, st_q = (at_re * st_p + at_im_p * st_q + inj_p,
                      at_re * st_q - at_im_p * st_p + inj_q)

    y = y + jnp.dot(xprev_ref[...].astype(BF16), sout_t, preferred_element_type=F32)
    y = y + dsk_ref[...] * x.astype(F32)

    yg = jax.nn.gelu(y, approximate=True)
    w_rows = jnp.dot(til_t, wglu_ref[...], precision=HIGHEST, preferred_element_type=F32)
    w_full = jnp.dot(w_rows, til, precision=HIGHEST, preferred_element_type=F32)
    same_step = (jnp.right_shift(lax.broadcasted_iota(jnp.int32, (th, th), 0), shift)
                 == jnp.right_shift(lax.broadcasted_iota(jnp.int32, (th, th), 1), shift))
    w_bd = jnp.where(same_step, w_full, 0.0).astype(BF16)
    gate = jnp.dot(yg.astype(BF16), w_bd, preferred_element_type=F32)
    z_ref[...] = (yg * jax.nn.sigmoid(gate)).astype(z_ref.dtype)


def s5_groups(u_t, lam_re, lam_im, log_dt, b_re, b_im, c_re, c_im, d_skip, w_glu, batch):
    groups, rows, th = u_t.shape
    t_len = th // S5_GROUP
    p2 = 2 * S5_STATE
    dup = lambda a, axis: jnp.concatenate([a, a], axis=axis)
    lrc = dup(lam_re, 1)[:, :, None]
    lic = dup(lam_im, 1)[:, :, None]
    lrr = dup(lam_re, 1)[:, None, :]
    lir = dup(lam_im, 1)[:, None, :]
    ldt = log_dt[:, None, None]
    c_re_t = jnp.swapaxes(c_re, 1, 2)
    c_im_t = jnp.swapaxes(c_im, 1, 2)
    cx = jnp.concatenate([c_re_t, c_im_t], axis=1)
    cy = jnp.concatenate([c_im_t, c_re_t], axis=1)
    btr = dup(jnp.swapaxes(b_re, 1, 2), 2)
    bti = dup(jnp.swapaxes(b_im, 1, 2), 2)
    dsk = jnp.tile(d_skip, (1, t_len))[:, None, :]

    def per_group(*tail):
        return pl.BlockSpec((None,) + tail, lambda g: (g,) + (0,) * len(tail))

    return pl.pallas_call(
        functools.partial(_s5_kernel, batch=batch),
        grid=(groups,),
        in_specs=[per_group(rows, th),
                  per_group(p2, 1), per_group(p2, 1), per_group(1, p2), per_group(1, p2),
                  per_group(1, 1), per_group(p2, S5_GROUP), per_group(p2, S5_GROUP),
                  per_group(S5_GROUP, p2), per_group(S5_GROUP, p2),
                  per_group(1, th), per_group(S5_GROUP, S5_GROUP)],
        out_specs=per_group(rows, th),
        out_shape=jax.ShapeDtypeStruct((groups, rows, th), BF16),
        scratch_shapes=[pltpu.VMEM((th, th), BF16),
                        pltpu.VMEM((rows, 2 * p2), F32),
                        pltpu.VMEM((rows, p2), F32)],
        compiler_params=_params("parallel"),
        name="s5_groups",
    )(u_t, lrc, lic, lrr, lir, ldt, cx, cy, btr, bti, dsk, w_glu)


def _rope_table_kernel(inv_ref, cos_ref, sin_ref):
    rows, width = cos_ref.shape
    pos = (pl.program_id(0) * rows
           + lax.broadcasted_iota(jnp.int32, (rows, width), 0)).astype(F32)
    ang = pos * inv_ref[...]
    first_half = lax.broadcasted_iota(jnp.int32, (rows, width), 1) < width // 2
    cos_ref[...] = jnp.cos(ang)
    sin_ref[...] = jnp.where(first_half, -jnp.sin(ang), jnp.sin(ang))


def rope_tables(seq):
    d = RET_HEAD_DIM
    inv_freq = ROPE_BASE ** (-jnp.arange(0, d, 2, dtype=F32) / d)
    inv2 = jnp.concatenate([inv_freq, inv_freq]).reshape(1, d)
    rows = min(seq, 512)
    return pl.pallas_call(
        _rope_table_kernel,
        grid=(seq // rows,),
        in_specs=[pl.BlockSpec((1, d), lambda i: (0, 0))],
        out_specs=[pl.BlockSpec((rows, d), lambda i: (i, 0))] * 2,
        out_shape=[jax.ShapeDtypeStruct((seq, d), F32)] * 2,
        compiler_params=_params("parallel"),
        name="rope_tables",
    )(inv2)


def _retention_kernel(q_ref, k_ref, v_ref, g_ref, cos_ref, sin_ref, gain_ref, o_ref, state_ref):
    tc, width = q_ref.shape
    d = RET_HEAD_DIM
    heads = width // d

    @pl.when(pl.program_id(1) == 0)
    def _():
        state_ref[...] = jnp.zeros_like(state_ref)

    cos = cos_ref[...]
    sin = sin_ref[...]
    n_col = lax.broadcasted_iota(jnp.int32, (tc, 1), 0).astype(F32)
    diff = (lax.broadcasted_iota(jnp.int32, (tc, tc), 0)
            - lax.broadcasted_iota(jnp.int32, (tc, tc), 1))
    causal = diff >= 0
    diff_f = jnp.where(causal, diff, 0).astype(F32)

    def rope(t):
        return t * cos + pltpu.roll(t, d // 2, axis=1) * sin

    for h in range(heads):
        log_g = math.log(1.0 - 2.0 ** (-5.0 - h))
        sl = slice(h * d, (h + 1) * d)
        q = rope(q_ref[:, sl])
        k = rope(k_ref[:, sl] * (d ** -0.5))
        v = v_ref[:, sl].astype(BF16)
        mask = jnp.where(causal, jnp.exp(diff_f * log_g), 0.0)
        scores = lax.dot_general(q.astype(BF16), k.astype(BF16), (((1,), (1,)), ((), ())),
                                 preferred_element_type=F32) * mask
        inner = jnp.dot(scores.astype(BF16), v, preferred_element_type=F32)
        state = state_ref[h]
        q_dec = q * jnp.exp((n_col + 1.0) * log_g)
        cross = jnp.dot(q_dec.astype(BF16), state.astype(BF16), preferred_element_type=F32)
        k_dec = k * jnp.exp((tc - 1.0 - n_col) * log_g)
        kv = lax.dot_general(k_dec.astype(BF16), v, (((0,), (0,)), ((), ())),
                             preferred_element_type=F32)
        state_ref[h] = math.exp(tc * log_g) * state + kv
        o = inner + cross
        mu = jnp.mean(o, axis=-1, keepdims=True)
        var = jnp.mean(jnp.square(o - mu), axis=-1, keepdims=True)
        o = (o - mu) * lax.rsqrt(var + GN_EPS) * gain_ref[:, sl]
        o_ref[:, sl] = (jax.nn.silu(g_ref[:, sl]) * o).astype(o_ref.dtype)


def retention(proj, cos_t, sin_t, gain, batch, seq, width, col0):
    tc = min(RET_CHUNK, seq)
    nc = seq // tc
    d = RET_HEAD_DIM

    def col_block(c):
        return pl.BlockSpec((tc, width), lambda b, i: (b * nc + i, c))

    return pl.pallas_call(
        _retention_kernel,
        grid=(batch, nc),
        in_specs=[col_block(col0), col_block(col0 + 1), col_block(col0 + 2), col_block(col0 + 3),
                  pl.BlockSpec((tc, d), lambda b, i: (i, 0)),
                  pl.BlockSpec((tc, d), lambda b, i: (i, 0)),
                  pl.BlockSpec((1, width), lambda b, i: (0, 0))],
        out_specs=pl.BlockSpec((tc, width), lambda b, i: (b * nc + i, 0)),
        out_shape=jax.ShapeDtypeStruct((batch * seq, width), BF16),
        scratch_shapes=[pltpu.VMEM((width // d, d, d), F32)],
        compiler_params=_params("parallel", "arbitrary"),
        name="retention",
    )(proj, proj, proj, proj, cos_t, sin_t, gain.reshape(1, width))


def _out_proj_kernel(s5_ref, ret_ref, gs_ref, w_ref, x_ref, o_ref, a_ref):
    @pl.when(pl.program_id(1) == 0)
    def _():
        half = s5_ref.shape[1]
        a_ref[:, :half] = (_rms_scale(s5_ref[...].astype(F32)) * gs_ref[...]).astype(BF16)
        a_ref[:, half:] = ret_ref[...]

    o_ref[...] = x_ref[...] + jnp.dot(a_ref[...], w_ref[...], preferred_element_type=F32)


def out_proj(y_s5, y_ret, s5_gain, w, x):
    m, d_s5 = y_s5.shape
    d_ret = y_ret.shape[1]
    k, n = w.shape
    tm, tn = min(ROW_TILE, m), min(COL_TILE, n)
    return pl.pallas_call(
        _out_proj_kernel,
        grid=(m // tm, n // tn),
        in_specs=[pl.BlockSpec((tm, d_s5), lambda i, j: (i, 0)),
                  pl.BlockSpec((tm, d_ret), lambda i, j: (i, 0)),
                  pl.BlockSpec((1, d_s5), lambda i, j: (0, 0)),
                  pl.BlockSpec((k, tn), lambda i, j: (0, j)),
                  pl.BlockSpec((tm, tn), lambda i, j: (i, j))],
        out_specs=pl.BlockSpec((tm, tn), lambda i, j: (i, j)),
        out_shape=jax.ShapeDtypeStruct((m, n), F32),
        scratch_shapes=[pltpu.VMEM((tm, k), BF16)],
        compiler_params=_params("parallel", "arbitrary"),
        name="out_proj",
    )(y_s5, y_ret, s5_gain.reshape(1, d_s5), w, x)


def _swiglu_accumulate(x_ref, ln_ref, wg_ref, wu_ref, wd_ref, h_ref, acc_ref, active):
    @pl.when(pl.program_id(1) == 0)
    def _():
        h_ref[...] = (_rms_scale(x_ref[...]) * ln_ref[...]).astype(BF16)
        acc_ref[...] = jnp.zeros_like(acc_ref)

    @pl.when(active)
    def _():
        h = h_ref[...]
        gate = jnp.dot(h, wg_ref[...], preferred_element_type=F32)
        up = jnp.dot(h, wu_ref[...], preferred_element_type=F32)
        act = (jax.nn.silu(gate) * up).astype(BF16)
        acc_ref[...] += jnp.dot(act, wd_ref[...], preferred_element_type=F32)


def _dense_ffn_kernel(x_ref, ln_ref, wg_ref, wu_ref, wd_ref, o_ref, h_ref, acc_ref):
    _swiglu_accumulate(x_ref, ln_ref, wg_ref, wu_ref, wd_ref, h_ref, acc_ref, True)

    @pl.when(pl.program_id(1) == pl.num_programs(1) - 1)
    def _():
        o_ref[...] = x_ref[...] + acc_ref[...]


def dense_ffn(x, ln, wg, wu, wd):
    m, d = x.shape
    f = wg.shape[1]
    tm, tf = min(FFN_ROW_TILE, m), min(FFN_HID_TILE, f)
    return pl.pallas_call(
        _dense_ffn_kernel,
        grid=(m // tm, f // tf),
        in_specs=[pl.BlockSpec((tm, d), lambda i, j: (i, 0)),
                  pl.BlockSpec((1, d), lambda i, j: (0, 0)),
                  pl.BlockSpec((d, tf), lambda i, j: (0, j)),
                  pl.BlockSpec((d, tf), lambda i, j: (0, j)),
                  pl.BlockSpec((tf, d), lambda i, j: (j, 0))],
        out_specs=pl.BlockSpec((tm, d), lambda i, j: (i, 0)),
        out_shape=jax.ShapeDtypeStruct((m, d), F32),
        scratch_shapes=[pltpu.VMEM((tm, d), BF16), pltpu.VMEM((tm, d), F32)],
        compiler_params=_params("parallel", "arbitrary"),
        name="dense_ffn",
    )(x, ln.reshape(1, d), wg, wu, wd)


def _expert_ffn_kernel(tile_expert_ref, tile_valid_ref, x_ref, gate_ref, ln_ref,
                       wg_ref, wu_ref, wd_ref, o_ref, h_ref, acc_ref):
    del tile_expert_ref
    valid = tile_valid_ref[pl.program_id(0)] > 0
    _swiglu_accumulate(x_ref, ln_ref, wg_ref, wu_ref, wd_ref, h_ref, acc_ref, valid)

    @pl.when(pl.program_id(1) == pl.num_programs(1) - 1)
    def _():
        o_ref[...] = gate_ref[...] * acc_ref[...]


def expert_ffn(xs, gate_sorted, tile_expert, tile_valid, ln, wg, wu, wd):
    r, d = xs.shape
    f = wg.shape[2]
    tm, tf = FFN_ROW_TILE, min(FFN_HID_TILE, f)
    nj = f // tf

    def hid(j, tv, i):
        return jnp.where(tv[i] > 0, j, nj - 1)

    grid_spec = pltpu.PrefetchScalarGridSpec(
        num_scalar_prefetch=2,
        grid=(r // tm, nj),
        in_specs=[pl.BlockSpec((tm, d), lambda i, j, te, tv: (i, 0)),
                  pl.BlockSpec((tm, 1), lambda i, j, te, tv: (i, 0)),
                  pl.BlockSpec((1, d), lambda i, j, te, tv: (0, 0)),
                  pl.BlockSpec((None, d, tf), lambda i, j, te, tv: (te[i], 0, hid(j, tv, i))),
                  pl.BlockSpec((None, d, tf), lambda i, j, te, tv: (te[i], 0, hid(j, tv, i))),
                  pl.BlockSpec((None, tf, d), lambda i, j, te, tv: (te[i], hid(j, tv, i), 0))],
        out_specs=pl.BlockSpec((tm, d), lambda i, j, te, tv: (i, 0)),
        scratch_shapes=[pltpu.VMEM((tm, d), BF16), pltpu.VMEM((tm, d), F32)])
    return pl.pallas_call(
        _expert_ffn_kernel,
        grid_spec=grid_spec,
        out_shape=jax.ShapeDtypeStruct((r, d), F32),
        compiler_params=_params("parallel", "arbitrary"),
        name="expert_ffn",
    )(tile_expert, tile_valid, xs, gate_sorted.reshape(r, 1), ln.reshape(1, d), wg, wu, wd)


def _router_kernel(x_ref, ln_ref, w_ref, b_ref, gates_ref, sel_ref, *, n_experts):
    h = _rms_scale(x_ref[...]) * ln_ref[...]
    logits = jnp.dot(h, w_ref[...], precision=HIGHEST, preferred_element_type=F32) + b_ref[...]
    lane = lax.broadcasted_iota(jnp.int32, logits.shape, 1)
    width = logits.shape[1]
    logits = jnp.where(lane < n_experts, logits, -jnp.inf)
    top1 = jnp.max(logits, axis=-1, keepdims=True)
    idx1 = jnp.min(jnp.where(logits == top1, lane, width), axis=-1, keepdims=True)
    rest = jnp.where(lane == idx1, -jnp.inf, logits)
    top2 = jnp.max(rest, axis=-1, keepdims=True)
    idx2 = jnp.min(jnp.where(rest == top2, lane, width), axis=-1, keepdims=True)
    e2 = jnp.exp(top2 - top1)
    denom = 1.0 + e2
    gates_ref[...] = jnp.where(lane == idx1, 1.0 / denom, 0.0) + jnp.where(lane == idx2, e2 / denom, 0.0)
    sel_ref[...] = jnp.where(lane == 0, idx1, jnp.where(lane == 1, idx2, 0))


def router(x, ln, w, b):
    m, d = x.shape
    n_experts = w.shape[1]
    lanes = 128
    w_pad = jnp.pad(w.astype(F32), ((0, 0), (0, lanes - n_experts)))
    b_pad = jnp.pad(b.astype(F32), (0, lanes - n_experts)).reshape(1, lanes)
    tm = min(ROW_TILE, m)
    return pl.pallas_call(
        functools.partial(_router_kernel, n_experts=n_experts),
        grid=(m // tm,),
        in_specs=[pl.BlockSpec((tm, d), lambda i: (i, 0)),
                  pl.BlockSpec((1, d), lambda i: (0, 0)),
                  pl.BlockSpec((d, lanes), lambda i: (0, 0)),
                  pl.BlockSpec((1, lanes), lambda i: (0, 0))],
        out_specs=[pl.BlockSpec((tm, lanes), lambda i: (i, 0))] * 2,
        out_shape=[jax.ShapeDtypeStruct((m, lanes), F32),
                   jax.ShapeDtypeStruct((m, lanes), jnp.int32)],
        compiler_params=_params("parallel"),
        name="router",
    )(x, ln.reshape(1, d), w_pad, b_pad)


def dispatch_plan(gates, sel, n_experts, tm):
    m = gates.shape[0]
    e_idx = sel[:, :TOP_K]
    w = jnp.take_along_axis(gates[:, :n_experts], e_idx, axis=1)
    flat_e = e_idx.reshape(-1)
    onehot = (flat_e[:, None] == jnp.arange(n_experts, dtype=jnp.int32)[None, :]).astype(jnp.int32)
    csum = jnp.cumsum(onehot, axis=0)
    rank = jnp.take_along_axis(csum, flat_e[:, None], axis=1)[:, 0] - 1
    counts = csum[-1]
    padded = ((counts + tm - 1) // tm) * tm
    ends = jnp.cumsum(padded)
    starts = ends - padded
    dest = (starts[flat_e] + rank).astype(jnp.int32)
    r = TOP_K * m + n_experts * tm
    src_token = jnp.zeros((r,), jnp.int32).at[dest].set(jnp.arange(TOP_K * m, dtype=jnp.int32) // TOP_K)
    gate_sorted = jnp.zeros((r,), F32).at[dest].set(w.reshape(-1))
    tile_start = jnp.arange(r // tm, dtype=jnp.int32) * tm
    tile_valid = (tile_start < ends[-1]).astype(jnp.int32)
    probe = jnp.minimum(tile_start, ends[-1] - 1)
    tile_expert = jnp.minimum(jnp.searchsorted(ends, probe, side="right"), n_experts - 1).astype(jnp.int32)
    return src_token, gate_sorted, dest, tile_expert, tile_valid


def _row_copy(src_hbm, dst_ref, src_row, dst_row, sem):
    return pltpu.make_async_copy(src_hbm.at[pl.ds(src_row, 1)], dst_ref.at[pl.ds(dst_row, 1)], sem)


def _row_gather_kernel(idx_ref, x_hbm, o_hbm, sems):
    window = GATHER_WINDOW
    n_windows = o_hbm.shape[0] // window

    def start_window(wi, slot):
        def body(r, carry):
            row = wi * window + r
            _row_copy(x_hbm, o_hbm, idx_ref[row], row, sems.at[slot]).start()
            return carry
        lax.fori_loop(0, window, body, 0, unroll=8)

    def wait_window(slot):
        def body(r, carry):
            _row_copy(x_hbm, o_hbm, 0, 0, sems.at[slot]).wait()
            return carry
        lax.fori_loop(0, window, body, 0, unroll=8)

    start_window(0, 0)

    def outer(wi, carry):
        slot = wi % 2

        @pl.when(wi + 1 < n_windows)
        def _():
            start_window(wi + 1, 1 - slot)

        wait_window(slot)
        return carry

    lax.fori_loop(0, n_windows, outer, 0)


def row_gather(x, idx):
    r = idx.shape[0]
    d = x.shape[1]
    assert r % GATHER_WINDOW == 0
    grid_spec = pltpu.PrefetchScalarGridSpec(
        num_scalar_prefetch=1,
        grid=(1,),
        in_specs=[pl.BlockSpec(memory_space=pl.ANY)],
        out_specs=pl.BlockSpec(memory_space=pl.ANY),
        scratch_shapes=[pltpu.SemaphoreType.DMA((2,))])
    return pl.pallas_call(
        _row_gather_kernel,
        grid_spec=grid_spec,
        out_shape=jax.ShapeDtypeStruct((r, d), x.dtype),
        compiler_params=_params("arbitrary"),
        name="row_gather",
    )(idx, x)


def _combine_kernel(pos_ref, x_ref, ys_hbm, gain_ref, o_ref, buf, sems, *, final_norm):
    i = pl.program_id(0)
    tc = x_ref.shape[0]

    def start(step, slot):
        def body(r, carry):
            t = step * tc + r
            for k in range(TOP_K):
                _row_copy(ys_hbm, buf.at[slot, k], pos_ref[TOP_K * t + k], r, sems.at[slot]).start()
            return carry
        lax.fori_loop(0, tc, body, 0, unroll=4)

    def wait(slot):
        def body(r, carry):
            for k in range(TOP_K):
                _row_copy(ys_hbm, buf.at[slot, k], 0, 0, sems.at[slot]).wait()
            return carry
        lax.fori_loop(0, tc, body, 0, unroll=4)

    slot = i % 2

    @pl.when(i == 0)
    def _():
        start(0, 0)

    @pl.when(i + 1 < pl.num_programs(0))
    def _():
        start(i + 1, 1 - slot)

    wait(slot)
    out = x_ref[...] + (buf[slot, 0] + buf[slot, 1])
    if final_norm:
        out = _rms_scale(out) * gain_ref[...]
    o_ref[...] = out


def moe_combine(x, ys, dest, final_gain):
    m, d = x.shape
    tc = min(256, m)
    final_norm = final_gain is not None
    gain = (final_gain if final_norm else jnp.ones((d,), F32)).reshape(1, d)
    grid_spec = pltpu.PrefetchScalarGridSpec(
        num_scalar_prefetch=1,
        grid=(m // tc,),
        in_specs=[pl.BlockSpec((tc, d), lambda i, pos: (i, 0)),
                  pl.BlockSpec(memory_space=pl.ANY),
                  pl.BlockSpec((1, d), lambda i, pos: (0, 0))],
        out_specs=pl.BlockSpec((tc, d), lambda i, pos: (i, 0)),
        scratch_shapes=[pltpu.VMEM((2, TOP_K, tc, d), F32), pltpu.SemaphoreType.DMA((2,))])
    return pl.pallas_call(
        functools.partial(_combine_kernel, final_norm=final_norm),
        grid_spec=grid_spec,
        out_shape=jax.ShapeDtypeStruct((m, d), F32),
        compiler_params=_params("arbitrary"),
        name="moe_combine",
    )(dest, x, ys, gain)


def moe_ffn(x, ln, router_w, router_b, wg, wu, wd, final_gain):
    n_experts = router_w.shape[1]
    gates, sel = router(x, ln, router_w, router_b)
    src_token, gate_sorted, dest, tile_expert, tile_valid = dispatch_plan(
        gates, sel, n_experts, FFN_ROW_TILE)
    xs = row_gather(x, src_token)
    ys = expert_ffn(xs, gate_sorted, tile_expert, tile_valid, ln, wg, wu, wd)
    return moe_combine(x, ys, dest, final_gain)


def _final_norm_kernel(x_ref, g_ref, o_ref):
    o_ref[...] = _rms_scale(x_ref[...]) * g_ref[...]


def final_norm(x, gain):
    m, d = x.shape
    tm = min(ROW_TILE, m)
    return pl.pallas_call(
        _final_norm_kernel,
        grid=(m // tm,),
        in_specs=[pl.BlockSpec((tm, d), lambda i: (i, 0)), pl.BlockSpec((1, d), lambda i: (0, 0))],
        out_specs=pl.BlockSpec((tm, d), lambda i: (i, 0)),
        out_shape=jax.ShapeDtypeStruct((m, d), F32),
        compiler_params=_params("parallel"),
        name="final_norm",
    )(x, gain.reshape(1, d))


def kernel(x, ln_mix, ln_ffn, w_in, w_out, s5_lam_re, s5_lam_im, s5_log_dt, s5_b_re, s5_b_im,
           s5_c_re, s5_c_im, s5_d, s5_w_glu, s5_gain, ret_gain, ffn_w_gate, ffn_w_up, ffn_w_down,
           router_w, router_b, moe_w_gate, moe_w_up, moe_w_down, final_gain):
    batch, seq, d_model = x.shape
    depth = ln_mix.shape[0]
    d_s5, d_ret = s5_gain.shape[1], ret_gain.shape[1]
    groups = s5_lam_re.shape[1]
    t_len = min(S5_CHUNK, seq)
    n_chunks = seq // t_len
    tokens = batch * seq

    cos_t, sin_t = rope_tables(seq)
    xf = x.reshape(tokens, d_model).astype(F32)
    for i in range(depth):
        proj = norm_matmul(xf, ln_mix[i], w_in[i].astype(BF16), F32)
        u_t = (proj[:, :d_s5].reshape(batch, n_chunks, t_len, groups, S5_GROUP)
               .transpose(3, 1, 0, 2, 4).reshape(groups, n_chunks * batch, t_len * S5_GROUP)
               .astype(BF16))
        z = s5_groups(u_t, s5_lam_re[i], s5_lam_im[i], s5_log_dt[i], s5_b_re[i], s5_b_im[i],
                      s5_c_re[i], s5_c_im[i], s5_d[i], s5_w_glu[i], batch)
        y_s5 = (z.reshape(groups, n_chunks, batch, t_len, S5_GROUP)
                .transpose(2, 1, 3, 0, 4).reshape(tokens, d_s5))
        y_ret = retention(proj, cos_t, sin_t, ret_gain[i], batch, seq, d_ret, d_s5 // d_ret)
        xf = out_proj(y_s5, y_ret, s5_gain[i], w_out[i].astype(BF16), xf)
        j = i // 2
        last = i == depth - 1
        if i % 2 == 0:
            xf = dense_ffn(xf, ln_ffn[i], ffn_w_gate[j].astype(BF16), ffn_w_up[j].astype(BF16),
                           ffn_w_down[j].astype(BF16))
            if last:
                xf = final_norm(xf, final_gain)
        else:
            xf = moe_ffn(xf, ln_ffn[i], router_w[j], router_b[j], moe_w_gate[j].astype(BF16),
                         moe_w_up[j].astype(BF16), moe_w_down[j].astype(BF16),
                         final_gain if last else None)
    return xf.reshape(batch, seq, d_model).astype(x.dtype)
```

```python
import functools
import math

import jax
import jax.numpy as jnp
from jax import lax
from jax.experimental import pallas as pl
from jax.experimental.pallas import tpu as pltpu

F32 = jnp.float32
BF16 = jnp.bfloat16
HIGHEST = lax.Precision.HIGHEST

S5_GROUP = 16
S5_STATE = 64
S5_LAM_RE_MAX = -1e-4
RET_HEAD_DIM = 128
ROPE_BASE = 10000.0
TOP_K = 2
NORM_EPS = 1e-6
GN_EPS = 1e-5

VMEM_LIMIT_BYTES = 56 * 1024 * 1024

S5_CHUNK = 64
RET_CHUNK = 256
ROW_TILE = 1024
COL_TILE = 1024
FFN_ROW_TILE = 512
FFN_HID_TILE = 512
EXPERT_ROW_TILE = 1024
EXPERT_SUB_ROWS = 512
EXPERT_HID_TILE = 512


def _params(*semantics):
    return pltpu.CompilerParams(dimension_semantics=semantics,
                                vmem_limit_bytes=VMEM_LIMIT_BYTES)


def _rms_scale(x):
    return x * lax.rsqrt(jnp.mean(x * x, axis=-1, keepdims=True) + NORM_EPS)


def _norm_matmul_kernel(x_ref, g_ref, w_ref, o_ref, h_ref):
    @pl.when(pl.program_id(1) == 0)
    def _():
        h_ref[...] = (_rms_scale(x_ref[...]) * g_ref[...]).astype(BF16)

    o_ref[...] = jnp.dot(h_ref[...], w_ref[...],
                         preferred_element_type=F32).astype(o_ref.dtype)


def norm_matmul(x, gain, w, out_dtype):
    m, k = x.shape
    n = w.shape[1]
    tm, tn = min(ROW_TILE, m), min(COL_TILE, n)
    return pl.pallas_call(
        _norm_matmul_kernel,
        grid=(m // tm, n // tn),
        in_specs=[pl.BlockSpec((tm, k), lambda i, j: (i, 0)),
                  pl.BlockSpec((1, k), lambda i, j: (0, 0)),
                  pl.BlockSpec((k, tn), lambda i, j: (0, j))],
        out_specs=pl.BlockSpec((tm, tn), lambda i, j: (i, j)),
        out_shape=jax.ShapeDtypeStruct((m, n), out_dtype),
        scratch_shapes=[pltpu.VMEM((tm, k), BF16)],
        compiler_params=_params("parallel", "arbitrary"),
        name="norm_matmul",
    )(x, gain.reshape(1, k), w)


def _bf16_terms(a):
    hi = a.astype(BF16)
    rest = a - hi.astype(F32)
    mid = rest.astype(BF16)
    lo = (rest - mid.astype(F32)).astype(BF16)
    return hi, mid, lo


def _select_dot(a, sel):
    return sum(jnp.dot(t, sel, preferred_element_type=F32) for t in _bf16_terms(a))


def _dot_select(sel, a):
    return sum(jnp.dot(sel, t, preferred_element_type=F32) for t in _bf16_terms(a))


def _s5_kernel(u_ref, lrc_ref, lic_ref, lrr_ref, lir_ref, ldt_ref, cx_ref, cy_ref,
               btr_ref, bti_ref, dsk_ref, wglu_ref, z_ref,
               mt_ref, s_ref, xprev_ref, *, batch):
    rows, th = u_ref.shape
    hh = S5_GROUP
    t_len = th // hh
    p2 = 2 * S5_STATE
    n_chunks = rows // batch
    shift = hh.bit_length() - 1

    dt = jnp.exp(ldt_ref[...])

    lrc = jnp.minimum(lrc_ref[...], S5_LAM_RE_MAX) * dt
    lic = lic_ref[...] * dt
    jv = lax.broadcasted_iota(jnp.int32, (p2, t_len), 1).astype(F32)

    ar0 = jnp.exp(lrc * jv) * jnp.cos(lic * jv)
    ai0 = jnp.exp(lrc * jv) * jnp.sin(lic * jv)

    lane_th = lax.broadcasted_iota(jnp.int32, (t_len, th), 1)
    rep = jnp.where(jnp.right_shift(lane_th, shift)
                    == lax.broadcasted_iota(jnp.int32, (t_len, th), 0), 1.0, 0.0).astype(BF16)
    til = jnp.where(jnp.bitwise_and(lax.broadcasted_iota(jnp.int32, (hh, th), 1), hh - 1)
                    == lax.broadcasted_iota(jnp.int32, (hh, th), 0), 1.0, 0.0).astype(BF16)

    row_p = lax.broadcasted_iota(jnp.int32, (p2, 1), 0)
    top = row_p < S5_STATE
    cx = _select_dot(cx_ref[...], til)
    cy = _select_dot(cy_ref[...], til)
    cy = jnp.where(top, -cy, cy)

    ar0_e = _select_dot(ar0, rep)
    ai0_e = _select_dot(ai0, rep)
    a1_re = jnp.exp(lrc) * jnp.cos(lic)
    a1_im = jnp.exp(lrc) * jnp.sin(lic)
    ar1_e = ar0_e * a1_re - ai0_e * a1_im
    ai1_e = ar0_e * a1_im + ai0_e * a1_re
    ca0 = ar0_e * cx + ai0_e * cy
    ca1 = ar1_e * cx + ai1_e * cy
    sout_t = jnp.where(top, ca1, -ca1).astype(BF16)

    lrr = jnp.minimum(lrr_ref[...], S5_LAM_RE_MAX)
    lir = lir_ref[...]
    mag = jnp.exp(lrr * dt)
    ab_re = mag * jnp.cos(lir * dt)
    ab_im = mag * jnp.sin(lir * dt)
    num_re = ab_re - 1.0
    den = lrr * lrr + lir * lir
    f_re = (num_re * lrr + ab_im * lir) / den
    f_im = (ab_im * lrr - num_re * lir) / den
    b_re = btr_ref[...]
    b_im = bti_ref[...]
    bb_re = f_re * b_re - f_im * b_im
    bb_im = f_re * b_im + f_im * b_re
    left = lax.broadcasted_iota(jnp.int32, (1, p2), 1) < S5_STATE

    kt = jnp.dot(jnp.where(left, bb_re, -bb_im), ca0,
                 precision=HIGHEST, preferred_element_type=F32)

    ktpad = jnp.concatenate([jnp.zeros_like(kt), kt], axis=1)
    per_vreg = 128 // hh
    rolled = [ktpad] + [pltpu.roll(ktpad, hh * r, axis=1) for r in range(1, per_vreg)]
    for s in range(t_len):
        q, r = divmod(s, per_vreg)
        mt_ref[pl.ds(s * hh, hh), :] = rolled[r][:, th - 128 * q: 2 * th - 128 * q].astype(BF16)

    sv = lax.broadcasted_iota(jnp.int32, (t_len, p2), 0).astype(F32)
    e_in = (t_len - 1.0) - sv
    mag_s = jnp.exp(lrr * dt * e_in)
    ang_s = lir * dt * e_in
    ars = mag_s * jnp.cos(ang_s)
    ais = mag_s * jnp.sin(ang_s)
    row_th = lax.broadcasted_iota(jnp.int32, (th, t_len), 0)
    rep_t = jnp.where(jnp.right_shift(row_th, shift)
                      == lax.broadcasted_iota(jnp.int32, (th, t_len), 1), 1.0, 0.0).astype(BF16)
    a_exp = _dot_select(rep_t, jnp.concatenate([ars, ais], axis=1))
    b_sel = jnp.concatenate([jnp.where(left, bb_re, bb_im), jnp.where(left, -bb_im, bb_re),
                             jnp.where(left, bb_im, bb_re), jnp.where(left, bb_re, -bb_im)],
                            axis=1)
    b_exp = jnp.tile(b_sel, (t_len, 1))
    ars_e, ais_e = a_exp[:, :p2], a_exp[:, p2:]
    sin_p = ars_e * b_exp[:, 0 * p2:1 * p2] + ais_e * b_exp[:, 1 * p2:2 * p2]
    sin_q = ars_e * b_exp[:, 2 * p2:3 * p2] + ais_e * b_exp[:, 3 * p2:4 * p2]
    sin_pq = jnp.concatenate([sin_p, sin_q], axis=1).astype(BF16)

    x = u_ref[...]
    y = jnp.dot(x, mt_ref[...], preferred_element_type=F32)
    s_ref[...] = jnp.dot(x, sin_pq, preferred_element_type=F32)

    at_mag = jnp.exp(lrr * dt * float(t_len))
    at_re = at_mag * jnp.cos(lir * dt * float(t_len))
    at_im = at_mag * jnp.sin(lir * dt * float(t_len))
    at_im_p = jnp.where(left, -at_im, at_im)
    st_p = jnp.zeros((batch, p2), F32)
    st_q = jnp.zeros((batch, p2), F32)
    for c in range(n_chunks):
        xprev_ref[pl.ds(c * batch, batch), :] = st_p
        inj_p = s_ref[pl.ds(c * batch, batch), 0:p2]
        inj_q = s_ref[pl.ds(c * batch, batch), p2:2 * p2]
        st_p, st_q = (at_re * st_p + at_im_p * st_q + inj_p,
                      at_re * st_q - at_im_p * st_p + inj_q)

    y = y + jnp.dot(xprev_ref[...].astype(BF16), sout_t, preferred_element_type=F32)
    y = y + dsk_ref[...] * x.astype(F32)

    yg = jax.nn.gelu(y, approximate=True)
    lanes = 128
    w_lane = _select_dot(wglu_ref[...], til[:, :lanes])
    w_blk = jnp.tile(w_lane, (per_vreg, 1))
    same_step = (jnp.right_shift(lax.broadcasted_iota(jnp.int32, (lanes, lanes), 0), shift)
                 == jnp.right_shift(lax.broadcasted_iota(jnp.int32, (lanes, lanes), 1), shift))
    w_blk = jnp.where(same_step, w_blk, 0.0).astype(BF16)
    yg_b = yg.astype(BF16)
    gate = jnp.concatenate(
        [jnp.dot(yg_b[:, q * lanes:(q + 1) * lanes], w_blk, preferred_element_type=F32)
         for q in range(th // lanes)], axis=1)
    z_ref[...] = (yg * jax.nn.sigmoid(gate)).astype(z_ref.dtype)


def s5_groups(u_t, lam_re, lam_im, log_dt, b_re, b_im, c_re, c_im, d_skip, w_glu, batch):
    groups, rows, th = u_t.shape
    t_len = th // S5_GROUP
    p2 = 2 * S5_STATE
    dup = lambda a, axis: jnp.concatenate([a, a], axis=axis)
    lrc = dup(lam_re, 1)[:, :, None]
    lic = dup(lam_im, 1)[:, :, None]
    lrr = dup(lam_re, 1)[:, None, :]
    lir = dup(lam_im, 1)[:, None, :]
    ldt = log_dt[:, None, None]
    c_re_t = jnp.swapaxes(c_re, 1, 2)
    c_im_t = jnp.swapaxes(c_im, 1, 2)
    cx = jnp.concatenate([c_re_t, c_im_t], axis=1)
    cy = jnp.concatenate([c_im_t, c_re_t], axis=1)
    btr = dup(jnp.swapaxes(b_re, 1, 2), 2)
    bti = dup(jnp.swapaxes(b_im, 1, 2), 2)
    dsk = jnp.tile(d_skip, (1, t_len))[:, None, :]

    def per_group(*tail):
        return pl.BlockSpec((None,) + tail, lambda g: (g,) + (0,) * len(tail))

    return pl.pallas_call(
        functools.partial(_s5_kernel, batch=batch),
        grid=(groups,),
        in_specs=[per_group(rows, th),
                  per_group(p2, 1), per_group(p2, 1), per_group(1, p2), per_group(1, p2),
                  per_group(1, 1), per_group(p2, S5_GROUP), per_group(p2, S5_GROUP),
                  per_group(S5_GROUP, p2), per_group(S5_GROUP, p2),
                  per_group(1, th), per_group(S5_GROUP, S5_GROUP)],
        out_specs=per_group(rows, th),
        out_shape=jax.ShapeDtypeStruct((groups, rows, th), BF16),
        scratch_shapes=[pltpu.VMEM((th, th), BF16),
                        pltpu.VMEM((rows, 2 * p2), F32),
                        pltpu.VMEM((rows, p2), F32)],
        compiler_params=_params("parallel"),
        name="s5_groups",
    )(u_t, lrc, lic, lrr, lir, ldt, cx, cy, btr, bti, dsk, w_glu)


def _rope_table_kernel(inv_ref, cos_ref, sin_ref):
    rows, width = cos_ref.shape
    pos = (pl.program_id(0) * rows
           + lax.broadcasted_iota(jnp.int32, (rows, width), 0)).astype(F32)
    ang = pos * inv_ref[...]
    first_half = lax.broadcasted_iota(jnp.int32, (rows, width), 1) < width // 2
    cos_ref[...] = jnp.cos(ang)
    sin_ref[...] = jnp.where(first_half, -jnp.sin(ang), jnp.sin(ang))


def rope_tables(seq):
    d = RET_HEAD_DIM
    inv_freq = ROPE_BASE ** (-jnp.arange(0, d, 2, dtype=F32) / d)
    inv2 = jnp.concatenate([inv_freq, inv_freq]).reshape(1, d)
    rows = min(seq, 512)
    return pl.pallas_call(
        _rope_table_kernel,
        grid=(seq // rows,),
        in_specs=[pl.BlockSpec((1, d), lambda i: (0, 0))],
        out_specs=[pl.BlockSpec((rows, d), lambda i: (i, 0))] * 2,
        out_shape=[jax.ShapeDtypeStruct((seq, d), F32)] * 2,
        compiler_params=_params("parallel"),
        name="rope_tables",
    )(inv2)


def _retention_kernel(q_ref, k_ref, v_ref, g_ref, cos_ref, sin_ref, gain_ref, o_ref, state_ref):
    tc, width = q_ref.shape
    d = RET_HEAD_DIM
    heads = width // d

    @pl.when(pl.program_id(1) == 0)
    def _():
        state_ref[...] = jnp.zeros_like(state_ref)

    cos = cos_ref[...]
    sin = sin_ref[...]
    n_col = lax.broadcasted_iota(jnp.int32, (tc, 1), 0).astype(F32)
    diff = (lax.broadcasted_iota(jnp.int32, (tc, tc), 0)
            - lax.broadcasted_iota(jnp.int32, (tc, tc), 1))
    causal = diff >= 0
    diff_f = jnp.where(causal, diff, 0).astype(F32)

    def rope(t):
        return t * cos + pltpu.roll(t, d // 2, axis=1) * sin

    for h in range(heads):
        log_g = math.log(1.0 - 2.0 ** (-5.0 - h))
        sl = slice(h * d, (h + 1) * d)
        q = rope(q_ref[:, sl])
        k = rope(k_ref[:, sl] * (d ** -0.5))
        v = v_ref[:, sl].astype(BF16)
        mask = jnp.where(causal, jnp.exp(diff_f * log_g), 0.0)
        scores = lax.dot_general(q.astype(BF16), k.astype(BF16), (((1,), (1,)), ((), ())),
                                 preferred_element_type=F32) * mask
        inner = jnp.dot(scores.astype(BF16), v, preferred_element_type=F32)
        state = state_ref[h]
        q_dec = q * jnp.exp((n_col + 1.0) * log_g)
        cross = jnp.dot(q_dec.astype(BF16), state.astype(BF16), preferred_element_type=F32)
        k_dec = k * jnp.exp((tc - 1.0 - n_col) * log_g)
        kv = lax.dot_general(k_dec.astype(BF16), v, (((0,), (0,)), ((), ())),
                             preferred_element_type=F32)
        state_ref[h] = math.exp(tc * log_g) * state + kv
        o = inner + cross
        mu = jnp.mean(o, axis=-1, keepdims=True)
        var = jnp.mean(jnp.square(o - mu), axis=-1, keepdims=True)
        o = (o - mu) * lax.rsqrt(var + GN_EPS) * gain_ref[:, sl]
        o_ref[:, sl] = (jax.nn.silu(g_ref[:, sl]) * o).astype(o_ref.dtype)


def retention(proj, cos_t, sin_t, gain, batch, seq, width, col0):
    tc = min(RET_CHUNK, seq)
    nc = seq // tc
    d = RET_HEAD_DIM

    def col_block(c):
        return pl.BlockSpec((tc, width), lambda b, i: (b * nc + i, c))

    return pl.pallas_call(
        _retention_kernel,
        grid=(batch, nc),
        in_specs=[col_block(col0), col_block(col0 + 1), col_block(col0 + 2), col_block(col0 + 3),
                  pl.BlockSpec((tc, d), lambda b, i: (i, 0)),
                  pl.BlockSpec((tc, d), lambda b, i: (i, 0)),
                  pl.BlockSpec((1, width), lambda b, i: (0, 0))],
        out_specs=pl.BlockSpec((tc, width), lambda b, i: (b * nc + i, 0)),
        out_shape=jax.ShapeDtypeStruct((batch * seq, width), BF16),
        scratch_shapes=[pltpu.VMEM((width // d, d, d), F32)],
        compiler_params=_params("parallel", "arbitrary"),
        name="retention",
    )(proj, proj, proj, proj, cos_t, sin_t, gain.reshape(1, width))


def _out_proj_kernel(s5_ref, ret_ref, gs_ref, w_ref, x_ref, o_ref, a_ref):
    @pl.when(pl.program_id(1) == 0)
    def _():
        half = s5_ref.shape[1]
        a_ref[:, :half] = (_rms_scale(s5_ref[...].astype(F32)) * gs_ref[...]).astype(BF16)
        a_ref[:, half:] = ret_ref[...]

    o_ref[...] = x_ref[...] + jnp.dot(a_ref[...], w_ref[...], preferred_element_type=F32)


def out_proj(y_s5, y_ret, s5_gain, w, x):
    m, d_s5 = y_s5.shape
    d_ret = y_ret.shape[1]
    k, n = w.shape
    tm, tn = min(ROW_TILE, m), min(COL_TILE, n)
    return pl.pallas_call(
        _out_proj_kernel,
        grid=(m // tm, n // tn),
        in_specs=[pl.BlockSpec((tm, d_s5), lambda i, j: (i, 0)),
                  pl.BlockSpec((tm, d_ret), lambda i, j: (i, 0)),
                  pl.BlockSpec((1, d_s5), lambda i, j: (0, 0)),
                  pl.BlockSpec((k, tn), lambda i, j: (0, j)),
                  pl.BlockSpec((tm, tn), lambda i, j: (i, j))],
        out_specs=pl.BlockSpec((tm, tn), lambda i, j: (i, j)),
        out_shape=jax.ShapeDtypeStruct((m, n), F32),
        scratch_shapes=[pltpu.VMEM((tm, k), BF16)],
        compiler_params=_params("parallel", "arbitrary"),
        name="out_proj",
    )(y_s5, y_ret, s5_gain.reshape(1, d_s5), w, x)


def _swiglu_accumulate(x_ref, ln_ref, wg_ref, wu_ref, wd_ref, h_ref, acc_ref, active):
    @pl.when(pl.program_id(1) == 0)
    def _():
        h_ref[...] = (_rms_scale(x_ref[...]) * ln_ref[...]).astype(BF16)
        acc_ref[...] = jnp.zeros_like(acc_ref)

    @pl.when(active)
    def _():
        h = h_ref[...]
        gate = jnp.dot(h, wg_ref[...], preferred_element_type=F32)
        up = jnp.dot(h, wu_ref[...], preferred_element_type=F32)
        act = (jax.nn.silu(gate) * up).astype(BF16)
        acc_ref[...] += jnp.dot(act, wd_ref[...], preferred_element_type=F32)


def _dense_ffn_kernel(x_ref, ln_ref, wg_ref, wu_ref, wd_ref, o_ref, h_ref, acc_ref):
    _swiglu_accumulate(x_ref, ln_ref, wg_ref, wu_ref, wd_ref, h_ref, acc_ref, True)

    @pl.when(pl.program_id(1) == pl.num_programs(1) - 1)
    def _():
        o_ref[...] = x_ref[...] + acc_ref[...]


def dense_ffn(x, ln, wg, wu, wd):
    m, d = x.shape
    f = wg.shape[1]
    tm, tf = min(FFN_ROW_TILE, m), min(FFN_HID_TILE, f)
    return pl.pallas_call(
        _dense_ffn_kernel,
        grid=(m // tm, f // tf),
        in_specs=[pl.BlockSpec((tm, d), lambda i, j: (i, 0)),
                  pl.BlockSpec((1, d), lambda i, j: (0, 0)),
                  pl.BlockSpec((d, tf), lambda i, j: (0, j)),
                  pl.BlockSpec((d, tf), lambda i, j: (0, j)),
                  pl.BlockSpec((tf, d), lambda i, j: (j, 0))],
        out_specs=pl.BlockSpec((tm, d), lambda i, j: (i, 0)),
        out_shape=jax.ShapeDtypeStruct((m, d), F32),
        scratch_shapes=[pltpu.VMEM((tm, d), BF16), pltpu.VMEM((tm, d), F32)],
        compiler_params=_params("parallel", "arbitrary"),
        name="dense_ffn",
    )(x, ln.reshape(1, d), wg, wu, wd)


def _row_copy(src_hbm, dst_ref, src_row, dst_row, sem):
    return pltpu.make_async_copy(src_hbm.at[pl.ds(src_row, 1)], dst_ref.at[pl.ds(dst_row, 1)], sem)


def _expert_ffn_kernel(tile_expert_ref, tile_rows_ref, src_ref, x_hbm, gate_ref, ln_ref,
                       wg_ref, wu_ref, wd_ref, ys_hbm, xbuf, h_ref, acc_ref, gather_sem, out_sem):
    del tile_expert_ref
    i, j = pl.program_id(0), pl.program_id(1)
    n_tiles, n_hid = pl.num_programs(0), pl.num_programs(1)
    tm = xbuf.shape[0]
    rows_valid = tile_rows_ref[i]

    def gather_start(tile):
        def body(r, carry):
            _row_copy(x_hbm, xbuf, src_ref[tile * tm + r], r, gather_sem).start()
            return carry
        lax.fori_loop(0, tm, body, 0, unroll=8)

    def gather_wait():
        def body(r, carry):
            _row_copy(x_hbm, xbuf, 0, 0, gather_sem).wait()
            return carry
        lax.fori_loop(0, tm, body, 0, unroll=8)

    @pl.when(j == 0)
    def _():
        @pl.when((i == 0) & (rows_valid > 0))
        def _():
            gather_start(0)

        @pl.when(rows_valid > 0)
        def _():
            gather_wait()
            h_ref[...] = (_rms_scale(xbuf[...]) * ln_ref[...]).astype(BF16)

        acc_ref[...] = jnp.zeros_like(acc_ref)

    @pl.when((j == 1) & (i + 1 < n_tiles))
    def _():
        @pl.when(tile_rows_ref[i + 1] > 0)
        def _():
            gather_start(i + 1)

    @pl.when(rows_valid > 0)
    def _():
        wg = wg_ref[...].astype(BF16)
        wu = wu_ref[...].astype(BF16)
        wd = wd_ref[...].astype(BF16)
        for s in range(tm // EXPERT_SUB_ROWS):
            @pl.when(rows_valid > s * EXPERT_SUB_ROWS)
            def _():
                rows = pl.ds(s * EXPERT_SUB_ROWS, EXPERT_SUB_ROWS)
                h = h_ref[rows, :]
                gate = jnp.dot(h, wg, preferred_element_type=F32)
                up = jnp.dot(h, wu, preferred_element_type=F32)
                act = (jax.nn.silu(gate) * up).astype(BF16)
                acc_ref[rows, :] += jnp.dot(act, wd, preferred_element_type=F32)

    @pl.when(j == n_hid - 1)
    def _():
        acc_ref[...] = gate_ref[...] * acc_ref[...]
        out_copy = pltpu.make_async_copy(
            acc_ref, ys_hbm.at[pl.ds(pl.multiple_of(i * tm, tm), tm)], out_sem)
        out_copy.start()
        out_copy.wait()


def expert_ffn(x, src_token, gate_sorted, tile_expert, tile_rows, ln, wg, wu, wd):
    d = x.shape[1]
    r = src_token.shape[0]
    f = wg.shape[2]
    tm, tf = EXPERT_ROW_TILE, min(EXPERT_HID_TILE, f)
    nj = f // tf
    assert nj >= 2 and r % tm == 0 and tm % EXPERT_SUB_ROWS == 0

    def hid(j, tr, i):
        return jnp.where(tr[i] > 0, j, nj - 1)

    grid_spec = pltpu.PrefetchScalarGridSpec(
        num_scalar_prefetch=3,
        grid=(r // tm, nj),
        in_specs=[pl.BlockSpec(memory_space=pl.ANY),
                  pl.BlockSpec((tm, 1), lambda i, j, te, tr, st: (i, 0)),
                  pl.BlockSpec((1, d), lambda i, j, te, tr, st: (0, 0)),
                  pl.BlockSpec((None, d, tf), lambda i, j, te, tr, st: (te[i], 0, hid(j, tr, i))),
                  pl.BlockSpec((None, d, tf), lambda i, j, te, tr, st: (te[i], 0, hid(j, tr, i))),
                  pl.BlockSpec((None, tf, d), lambda i, j, te, tr, st: (te[i], hid(j, tr, i), 0))],
        out_specs=pl.BlockSpec(memory_space=pl.ANY),
        scratch_shapes=[pltpu.VMEM((tm, d), F32), pltpu.VMEM((tm, d), BF16),
                        pltpu.VMEM((tm, d), F32),
                        pltpu.SemaphoreType.DMA(()), pltpu.SemaphoreType.DMA(())])
    return pl.pallas_call(
        _expert_ffn_kernel,
        grid_spec=grid_spec,
        out_shape=jax.ShapeDtypeStruct((r, d), F32),
        compiler_params=_params("arbitrary", "arbitrary"),
        name="expert_ffn",
    )(tile_expert, tile_rows, src_token, x, gate_sorted.reshape(r, 1), ln.reshape(1, d), wg, wu, wd)


def _router_kernel(x_ref, ln_ref, w_ref, b_ref, gates_ref, sel_ref, *, n_experts):
    h = _rms_scale(x_ref[...]) * ln_ref[...]
    logits = jnp.dot(h, w_ref[...], precision=HIGHEST, preferred_element_type=F32) + b_ref[...]
    lane = lax.broadcasted_iota(jnp.int32, logits.shape, 1)
    width = logits.shape[1]
    logits = jnp.where(lane < n_experts, logits, -jnp.inf)
    top1 = jnp.max(logits, axis=-1, keepdims=True)
    idx1 = jnp.min(jnp.where(logits == top1, lane, width), axis=-1, keepdims=True)
    rest = jnp.where(lane == idx1, -jnp.inf, logits)
    top2 = jnp.max(rest, axis=-1, keepdims=True)
    idx2 = jnp.min(jnp.where(rest == top2, lane, width), axis=-1, keepdims=True)
    e2 = jnp.exp(top2 - top1)
    denom = 1.0 + e2
    gates_ref[...] = jnp.where(lane == idx1, 1.0 / denom, 0.0) + jnp.where(lane == idx2, e2 / denom, 0.0)
    sel_ref[...] = jnp.where(lane == 0, idx1, jnp.where(lane == 1, idx2, 0))


def router(x, ln, w, b):
    m, d = x.shape
    n_experts = w.shape[1]
    lanes = 128
    w_pad = jnp.pad(w.astype(F32), ((0, 0), (0, lanes - n_experts)))
    b_pad = jnp.pad(b.astype(F32), (0, lanes - n_experts)).reshape(1, lanes)
    tm = min(ROW_TILE, m)
    return pl.pallas_call(
        functools.partial(_router_kernel, n_experts=n_experts),
        grid=(m // tm,),
        in_specs=[pl.BlockSpec((tm, d), lambda i: (i, 0)),
                  pl.BlockSpec((1, d), lambda i: (0, 0)),
                  pl.BlockSpec((d, lanes), lambda i: (0, 0)),
                  pl.BlockSpec((1, lanes), lambda i: (0, 0))],
        out_specs=[pl.BlockSpec((tm, lanes), lambda i: (i, 0))] * 2,
        out_shape=[jax.ShapeDtypeStruct((m, lanes), F32),
                   jax.ShapeDtypeStruct((m, lanes), jnp.int32)],
        compiler_params=_params("parallel"),
        name="router",
    )(x, ln.reshape(1, d), w_pad, b_pad)


def dispatch_plan(gates, sel, n_experts, tm):
    m = gates.shape[0]
    e_idx = sel[:, :TOP_K]
    w = jnp.take_along_axis(gates[:, :n_experts], e_idx, axis=1)
    flat_e = e_idx.reshape(-1)
    onehot = (flat_e[:, None] == jnp.arange(n_experts, dtype=jnp.int32)[None, :]).astype(jnp.int32)
    csum = jnp.cumsum(onehot, axis=0)
    rank = jnp.take_along_axis(csum, flat_e[:, None], axis=1)[:, 0] - 1
    counts = csum[-1]
    padded = ((counts + tm - 1) // tm) * tm
    ends = jnp.cumsum(padded)
    starts = ends - padded
    dest = (starts[flat_e] + rank).astype(jnp.int32)
    r = TOP_K * m + n_experts * tm
    src_token = jnp.zeros((r,), jnp.int32).at[dest].set(jnp.arange(TOP_K * m, dtype=jnp.int32) // TOP_K)
    gate_sorted = jnp.zeros((r,), F32).at[dest].set(w.reshape(-1))
    tile_start = jnp.arange(r // tm, dtype=jnp.int32) * tm
    probe = jnp.minimum(tile_start, ends[-1] - 1)
    tile_expert = jnp.minimum(jnp.sum((ends[None, :] <= probe[:, None]).astype(jnp.int32), axis=1),
                              n_experts - 1).astype(jnp.int32)
    real_end = (starts + counts)[tile_expert]
    tile_rows = jnp.where(tile_start < ends[-1], jnp.clip(real_end - tile_start, 0, tm), 0)
    return src_token, gate_sorted, dest, tile_expert, tile_rows.astype(jnp.int32)


def _combine_kernel(pos_ref, x_ref, ys_hbm, gain_ref, o_ref, buf, sems, *, final_norm):
    i = pl.program_id(0)
    tc = x_ref.shape[0]

    def start(step, slot):
        def body(r, carry):
            t = step * tc + r
            for k in range(TOP_K):
                _row_copy(ys_hbm, buf.at[slot, k], pos_ref[TOP_K * t + k], r, sems.at[slot]).start()
            return carry
        lax.fori_loop(0, tc, body, 0, unroll=4)

    def wait(slot):
        def body(r, carry):
            for k in range(TOP_K):
                _row_copy(ys_hbm, buf.at[slot, k], 0, 0, sems.at[slot]).wait()
            return carry
        lax.fori_loop(0, tc, body, 0, unroll=4)

    slot = i % 2

    @pl.when(i == 0)
    def _():
        start(0, 0)

    @pl.when(i + 1 < pl.num_programs(0))
    def _():
        start(i + 1, 1 - slot)

    wait(slot)
    out = x_ref[...] + (buf[slot, 0] + buf[slot, 1])
    if final_norm:
        out = _rms_scale(out) * gain_ref[...]
    o_ref[...] = out


def moe_combine(x, ys, dest, final_gain):
    m, d = x.shape
    tc = min(256, m)
    final_norm = final_gain is not None
    gain = (final_gain if final_norm else jnp.ones((d,), F32)).reshape(1, d)
    grid_spec = pltpu.PrefetchScalarGridSpec(
        num_scalar_prefetch=1,
        grid=(m // tc,),
        in_specs=[pl.BlockSpec((tc, d), lambda i, pos: (i, 0)),
                  pl.BlockSpec(memory_space=pl.ANY),
                  pl.BlockSpec((1, d), lambda i, pos: (0, 0))],
        out_specs=pl.BlockSpec((tc, d), lambda i, pos: (i, 0)),
        scratch_shapes=[pltpu.VMEM((2, TOP_K, tc, d), F32), pltpu.SemaphoreType.DMA((2,))])
    return pl.pallas_call(
        functools.partial(_combine_kernel, final_norm=final_norm),
        grid_spec=grid_spec,
        out_shape=jax.ShapeDtypeStruct((m, d), F32),
        compiler_params=_params("arbitrary"),
        name="moe_combine",
    )(dest, x, ys, gain)


def moe_ffn(x, ln, router_w, router_b, wg, wu, wd, final_gain):
    n_experts = router_w.shape[1]
    gates, sel = router(x, ln, router_w, router_b)
    src_token, gate_sorted, dest, tile_expert, tile_rows = dispatch_plan(
        gates, sel, n_experts, EXPERT_ROW_TILE)
    ys = expert_ffn(x, src_token, gate_sorted, tile_expert, tile_rows, ln, wg, wu, wd)
    return moe_combine(x, ys, dest, final_gain)


def _final_norm_kernel(x_ref, g_ref, o_ref):
    o_ref[...] = _rms_scale(x_ref[...]) * g_ref[...]


def final_norm(x, gain):
    m, d = x.shape
    tm = min(ROW_TILE, m)
    return pl.pallas_call(
        _final_norm_kernel,
        grid=(m // tm,),
        in_specs=[pl.BlockSpec((tm, d), lambda i: (i, 0)), pl.BlockSpec((1, d), lambda i: (0, 0))],
        out_specs=pl.BlockSpec((tm, d), lambda i: (i, 0)),
        out_shape=jax.ShapeDtypeStruct((m, d), F32),
        compiler_params=_params("parallel"),
        name="final_norm",
    )(x, gain.reshape(1, d))


def kernel(x, ln_mix, ln_ffn, w_in, w_out, s5_lam_re, s5_lam_im, s5_log_dt, s5_b_re, s5_b_im,
           s5_c_re, s5_c_im, s5_d, s5_w_glu, s5_gain, ret_gain, ffn_w_gate, ffn_w_up, ffn_w_down,
           router_w, router_b, moe_w_gate, moe_w_up, moe_w_down, final_gain):
    batch, seq, d_model = x.shape
    depth = ln_mix.shape[0]
    d_s5, d_ret = s5_gain.shape[1], ret_gain.shape[1]
    groups = s5_lam_re.shape[1]
    t_len = min(S5_CHUNK, seq)
    n_chunks = seq // t_len
    tokens = batch * seq

    cos_t, sin_t = rope_tables(seq)
    xf = x.reshape(tokens, d_model).astype(F32)
    for i in range(depth):
        proj = norm_matmul(xf, ln_mix[i], w_in[i].astype(BF16), F32)
        u_t = (proj[:, :d_s5].reshape(batch, n_chunks, t_len, groups, S5_GROUP)
               .transpose(3, 1, 0, 2, 4).reshape(groups, n_chunks * batch, t_len * S5_GROUP)
               .astype(BF16))
        z = s5_groups(u_t, s5_lam_re[i], s5_lam_im[i], s5_log_dt[i], s5_b_re[i], s5_b_im[i],
                      s5_c_re[i], s5_c_im[i], s5_d[i], s5_w_glu[i], batch)
        y_s5 = (z.reshape(groups, n_chunks, batch, t_len, S5_GROUP)
                .transpose(2, 1, 3, 0, 4).reshape(tokens, d_s5))
        y_ret = retention(proj, cos_t, sin_t, ret_gain[i], batch, seq, d_ret, d_s5 // d_ret)
        xf = out_proj(y_s5, y_ret, s5_gain[i], w_out[i].astype(BF16), xf)
        j = i // 2
        last = i == depth - 1
        if i % 2 == 0:
            xf = dense_ffn(xf, ln_ffn[i], ffn_w_gate[j].astype(BF16), ffn_w_up[j].astype(BF16),
                           ffn_w_down[j].astype(BF16))
            if last:
                xf = final_norm(xf, final_gain)
        else:
            xf = moe_ffn(xf, ln_ffn[i], router_w[j], router_b[j], moe_w_gate[j].astype(F32),
                         moe_w_up[j].astype(F32), moe_w_down[j].astype(F32),
                         final_gain if last else None)
    return xf.reshape(batch, seq, d_model).astype(x.dtype)
```

```python
import functools
import math

import jax
import jax.numpy as jnp
from jax import lax
from jax.experimental import pallas as pl
from jax.experimental.pallas import tpu as pltpu

F32 = jnp.float32
BF16 = jnp.bfloat16
HIGHEST = lax.Precision.HIGHEST

S5_GROUP = 16
S5_STATE = 64
S5_LAM_RE_MAX = -1e-4
RET_HEAD_DIM = 128
ROPE_BASE = 10000.0
TOP_K = 2
NORM_EPS = 1e-6
GN_EPS = 1e-5

VMEM_LIMIT_BYTES = 56 * 1024 * 1024

S5_CHUNK = 64
RET_CHUNK = 256
ROW_TILE = 1024
COL_TILE = 1024
FFN_ROW_TILE = 512
FFN_HID_TILE = 512
EXPERT_ROW_TILE = 1024
EXPERT_SUB_ROWS = 512
EXPERT_HID_TILE = 512


def _params(*semantics):
    return pltpu.CompilerParams(dimension_semantics=semantics,
                                vmem_limit_bytes=VMEM_LIMIT_BYTES)


def _rms_scale(x):
    return x * lax.rsqrt(jnp.mean(x * x, axis=-1, keepdims=True) + NORM_EPS)


def _norm_matmul_kernel(x_ref, g_ref, w_ref, o_ref, h_ref):
    @pl.when(pl.program_id(1) == 0)
    def _():
        h_ref[...] = (_rms_scale(x_ref[...]) * g_ref[...]).astype(BF16)

    o_ref[...] = jnp.dot(h_ref[...], w_ref[...],
                         preferred_element_type=F32).astype(o_ref.dtype)


def norm_matmul(x, gain, w, layer, out_dtype):
    m, k = x.shape
    n = w.shape[2]
    tm, tn = min(ROW_TILE, m), min(COL_TILE, n)
    return pl.pallas_call(
        _norm_matmul_kernel,
        grid=(m // tm, n // tn),
        in_specs=[pl.BlockSpec((tm, k), lambda i, j: (i, 0)),
                  pl.BlockSpec((1, k), lambda i, j: (0, 0)),
                  pl.BlockSpec((None, k, tn), lambda i, j: (layer, 0, j))],
        out_specs=pl.BlockSpec((tm, tn), lambda i, j: (i, j)),
        out_shape=jax.ShapeDtypeStruct((m, n), out_dtype),
        scratch_shapes=[pltpu.VMEM((tm, k), BF16)],
        compiler_params=_params("parallel", "arbitrary"),
        name="norm_matmul",
    )(x, gain.reshape(1, k), w)


def _bf16_terms(a):
    hi = a.astype(BF16)
    rest = a - hi.astype(F32)
    mid = rest.astype(BF16)
    lo = (rest - mid.astype(F32)).astype(BF16)
    return hi, mid, lo


def _select_dot(a, sel):
    return sum(jnp.dot(t, sel, preferred_element_type=F32) for t in _bf16_terms(a))


def _dot_select(sel, a):
    return sum(jnp.dot(sel, t, preferred_element_type=F32) for t in _bf16_terms(a))


def _s5_kernel(u_ref, lrc_ref, lic_ref, lrr_ref, lir_ref, ldt_ref, cx_ref, cy_ref,
               btr_ref, bti_ref, dsk_ref, wglu_ref, z_ref,
               mt_ref, s_ref, xprev_ref, *, batch):
    rows, th = u_ref.shape
    hh = S5_GROUP
    t_len = th // hh
    p2 = 2 * S5_STATE
    n_chunks = rows // batch
    shift = hh.bit_length() - 1

    dt = jnp.exp(ldt_ref[...])

    lrc = jnp.minimum(lrc_ref[...], S5_LAM_RE_MAX) * dt
    lic = lic_ref[...] * dt
    jv = lax.broadcasted_iota(jnp.int32, (p2, t_len), 1).astype(F32)

    ar0 = jnp.exp(lrc * jv) * jnp.cos(lic * jv)
    ai0 = jnp.exp(lrc * jv) * jnp.sin(lic * jv)

    lane_th = lax.broadcasted_iota(jnp.int32, (t_len, th), 1)
    rep = jnp.where(jnp.right_shift(lane_th, shift)
                    == lax.broadcasted_iota(jnp.int32, (t_len, th), 0), 1.0, 0.0).astype(BF16)
    til = jnp.where(jnp.bitwise_and(lax.broadcasted_iota(jnp.int32, (hh, th), 1), hh - 1)
                    == lax.broadcasted_iota(jnp.int32, (hh, th), 0), 1.0, 0.0).astype(BF16)

    row_p = lax.broadcasted_iota(jnp.int32, (p2, 1), 0)
    top = row_p < S5_STATE
    cx = _select_dot(cx_ref[...], til)
    cy = _select_dot(cy_ref[...], til)
    cy = jnp.where(top, -cy, cy)

    ar0_e = _select_dot(ar0, rep)
    ai0_e = _select_dot(ai0, rep)
    a1_re = jnp.exp(lrc) * jnp.cos(lic)
    a1_im = jnp.exp(lrc) * jnp.sin(lic)
    ar1_e = ar0_e * a1_re - ai0_e * a1_im
    ai1_e = ar0_e * a1_im + ai0_e * a1_re
    ca0 = ar0_e * cx + ai0_e * cy
    ca1 = ar1_e * cx + ai1_e * cy
    sout_t = jnp.where(top, ca1, -ca1).astype(BF16)

    lrr = jnp.minimum(lrr_ref[...], S5_LAM_RE_MAX)
    lir = lir_ref[...]
    mag = jnp.exp(lrr * dt)
    ab_re = mag * jnp.cos(lir * dt)
    ab_im = mag * jnp.sin(lir * dt)
    num_re = ab_re - 1.0
    den = lrr * lrr + lir * lir
    f_re = (num_re * lrr + ab_im * lir) / den
    f_im = (ab_im * lrr - num_re * lir) / den
    b_re = btr_ref[...]
    b_im = bti_ref[...]
    bb_re = f_re * b_re - f_im * b_im
    bb_im = f_re * b_im + f_im * b_re
    left = lax.broadcasted_iota(jnp.int32, (1, p2), 1) < S5_STATE

    kt = jnp.dot(jnp.where(left, bb_re, -bb_im), ca0,
                 precision=HIGHEST, preferred_element_type=F32)

    ktpad = jnp.concatenate([jnp.zeros_like(kt), kt], axis=1)
    per_vreg = 128 // hh
    rolled = [ktpad] + [pltpu.roll(ktpad, hh * r, axis=1) for r in range(1, per_vreg)]
    for s in range(t_len):
        q, r = divmod(s, per_vreg)
        mt_ref[pl.ds(s * hh, hh), :] = rolled[r][:, th - 128 * q: 2 * th - 128 * q].astype(BF16)

    sv = lax.broadcasted_iota(jnp.int32, (t_len, p2), 0).astype(F32)
    e_in = (t_len - 1.0) - sv
    mag_s = jnp.exp(lrr * dt * e_in)
    ang_s = lir * dt * e_in
    ars = mag_s * jnp.cos(ang_s)
    ais = mag_s * jnp.sin(ang_s)
    row_th = lax.broadcasted_iota(jnp.int32, (th, t_len), 0)
    rep_t = jnp.where(jnp.right_shift(row_th, shift)
                      == lax.broadcasted_iota(jnp.int32, (th, t_len), 1), 1.0, 0.0).astype(BF16)
    a_exp = _dot_select(rep_t, jnp.concatenate([ars, ais], axis=1))
    b_sel = jnp.concatenate([jnp.where(left, bb_re, bb_im), jnp.where(left, -bb_im, bb_re),
                             jnp.where(left, bb_im, bb_re), jnp.where(left, bb_re, -bb_im)],
                            axis=1)
    b_exp = jnp.tile(b_sel, (t_len, 1))
    ars_e, ais_e = a_exp[:, :p2], a_exp[:, p2:]
    sin_p = ars_e * b_exp[:, 0 * p2:1 * p2] + ais_e * b_exp[:, 1 * p2:2 * p2]
    sin_q = ars_e * b_exp[:, 2 * p2:3 * p2] + ais_e * b_exp[:, 3 * p2:4 * p2]
    sin_pq = jnp.concatenate([sin_p, sin_q], axis=1).astype(BF16)

    x = u_ref[...]
    y = jnp.dot(x, mt_ref[...], preferred_element_type=F32)
    s_ref[...] = jnp.dot(x, sin_pq, preferred_element_type=F32)

    at_mag = jnp.exp(lrr * dt * float(t_len))
    at_re = at_mag * jnp.cos(lir * dt * float(t_len))
    at_im = at_mag * jnp.sin(lir * dt * float(t_len))
    at_im_p = jnp.where(left, -at_im, at_im)
    st_p = jnp.zeros((batch, p2), F32)
    st_q = jnp.zeros((batch, p2), F32)
    for c in range(n_chunks):
        xprev_ref[pl.ds(c * batch, batch), :] = st_p
        inj_p = s_ref[pl.ds(c * batch, batch), 0:p2]
        inj_q = s_ref[pl.ds(c * batch, batch), p2:2 * p2]
        st_p, st_q = (at_re * st_p + at_im_p * st_q + inj_p,
                      at_re * st_q - at_im_p * st_p + inj_q)

    y = y + jnp.dot(xprev_ref[...].astype(BF16), sout_t, preferred_element_type=F32)
    y = y + dsk_ref[...] * x.astype(F32)

    yg = jax.nn.gelu(y, approximate=True)
    lanes = 128
    w_lane = _select_dot(wglu_ref[...], til[:, :lanes])
    w_blk = jnp.tile(w_lane, (per_vreg, 1))
    same_step = (jnp.right_shift(lax.broadcasted_iota(jnp.int32, (lanes, lanes), 0), shift)
                 == jnp.right_shift(lax.broadcasted_iota(jnp.int32, (lanes, lanes), 1), shift))
    w_blk = jnp.where(same_step, w_blk, 0.0).astype(BF16)
    yg_b = yg.astype(BF16)
    gate = jnp.concatenate(
        [jnp.dot(yg_b[:, q * lanes:(q + 1) * lanes], w_blk, preferred_element_type=F32)
         for q in range(th // lanes)], axis=1)
    z_ref[...] = (yg * jax.nn.sigmoid(gate)).astype(z_ref.dtype)


def s5_groups(u_t, lam_re, lam_im, log_dt, b_re, b_im, c_re, c_im, d_skip, w_glu, batch):
    groups, rows, th = u_t.shape
    t_len = th // S5_GROUP
    p2 = 2 * S5_STATE
    dup = lambda a, axis: jnp.concatenate([a, a], axis=axis)
    lrc = dup(lam_re, 1)[:, :, None]
    lic = dup(lam_im, 1)[:, :, None]
    lrr = dup(lam_re, 1)[:, None, :]
    lir = dup(lam_im, 1)[:, None, :]
    ldt = log_dt[:, None, None]
    c_re_t = jnp.swapaxes(c_re, 1, 2)
    c_im_t = jnp.swapaxes(c_im, 1, 2)
    cx = jnp.concatenate([c_re_t, c_im_t], axis=1)
    cy = jnp.concatenate([c_im_t, c_re_t], axis=1)
    btr = dup(jnp.swapaxes(b_re, 1, 2), 2)
    bti = dup(jnp.swapaxes(b_im, 1, 2), 2)
    dsk = jnp.tile(d_skip, (1, t_len))[:, None, :]

    def per_group(*tail):
        return pl.BlockSpec((None,) + tail, lambda g: (g,) + (0,) * len(tail))

    return pl.pallas_call(
        functools.partial(_s5_kernel, batch=batch),
        grid=(groups,),
        in_specs=[per_group(rows, th),
                  per_group(p2, 1), per_group(p2, 1), per_group(1, p2), per_group(1, p2),
                  per_group(1, 1), per_group(p2, S5_GROUP), per_group(p2, S5_GROUP),
                  per_group(S5_GROUP, p2), per_group(S5_GROUP, p2),
                  per_group(1, th), per_group(S5_GROUP, S5_GROUP)],
        out_specs=per_group(rows, th),
        out_shape=jax.ShapeDtypeStruct((groups, rows, th), BF16),
        scratch_shapes=[pltpu.VMEM((th, th), BF16),
                        pltpu.VMEM((rows, 2 * p2), F32),
                        pltpu.VMEM((rows, p2), F32)],
        compiler_params=_params("parallel"),
        name="s5_groups",
    )(u_t, lrc, lic, lrr, lir, ldt, cx, cy, btr, bti, dsk, w_glu)


def _rope_table_kernel(inv_ref, cos_ref, sin_ref):
    rows, width = cos_ref.shape
    pos = (pl.program_id(0) * rows
           + lax.broadcasted_iota(jnp.int32, (rows, width), 0)).astype(F32)
    ang = pos * inv_ref[...]
    first_half = lax.broadcasted_iota(jnp.int32, (rows, width), 1) < width // 2
    cos_ref[...] = jnp.cos(ang)
    sin_ref[...] = jnp.where(first_half, -jnp.sin(ang), jnp.sin(ang))


def rope_tables(seq):
    d = RET_HEAD_DIM
    inv_freq = ROPE_BASE ** (-jnp.arange(0, d, 2, dtype=F32) / d)
    inv2 = jnp.concatenate([inv_freq, inv_freq]).reshape(1, d)
    rows = min(seq, 512)
    return pl.pallas_call(
        _rope_table_kernel,
        grid=(seq // rows,),
        in_specs=[pl.BlockSpec((1, d), lambda i: (0, 0))],
        out_specs=[pl.BlockSpec((rows, d), lambda i: (i, 0))] * 2,
        out_shape=[jax.ShapeDtypeStruct((seq, d), F32)] * 2,
        compiler_params=_params("parallel"),
        name="rope_tables",
    )(inv2)


def _retention_kernel(q_ref, k_ref, v_ref, g_ref, cos_ref, sin_ref, gain_ref, o_ref, state_ref):
    tc, width = q_ref.shape
    d = RET_HEAD_DIM
    heads = width // d

    @pl.when(pl.program_id(1) == 0)
    def _():
        state_ref[...] = jnp.zeros_like(state_ref)

    cos = cos_ref[...]
    sin = sin_ref[...]
    n_col = lax.broadcasted_iota(jnp.int32, (tc, 1), 0).astype(F32)
    diff = (lax.broadcasted_iota(jnp.int32, (tc, tc), 0)
            - lax.broadcasted_iota(jnp.int32, (tc, tc), 1))
    causal = diff >= 0
    diff_f = jnp.where(causal, diff, 0).astype(F32)

    def rope(t):
        return t * cos + pltpu.roll(t, d // 2, axis=1) * sin

    for h in range(heads):
        log_g = math.log(1.0 - 2.0 ** (-5.0 - h))
        sl = slice(h * d, (h + 1) * d)
        q = rope(q_ref[:, sl].astype(F32))
        k = rope(k_ref[:, sl].astype(F32) * (d ** -0.5))
        v = v_ref[:, sl].astype(BF16)
        mask = jnp.where(causal, jnp.exp(diff_f * log_g), 0.0)
        scores = lax.dot_general(q.astype(BF16), k.astype(BF16), (((1,), (1,)), ((), ())),
                                 preferred_element_type=F32) * mask
        inner = jnp.dot(scores.astype(BF16), v, preferred_element_type=F32)
        state = state_ref[h]
        q_dec = q * jnp.exp((n_col + 1.0) * log_g)
        cross = jnp.dot(q_dec.astype(BF16), state.astype(BF16), preferred_element_type=F32)
        k_dec = k * jnp.exp((tc - 1.0 - n_col) * log_g)
        kv = lax.dot_general(k_dec.astype(BF16), v, (((0,), (0,)), ((), ())),
                             preferred_element_type=F32)
        state_ref[h] = math.exp(tc * log_g) * state + kv
        o = inner + cross
        mu = jnp.mean(o, axis=-1, keepdims=True)
        var = jnp.mean(jnp.square(o - mu), axis=-1, keepdims=True)
        o = (o - mu) * lax.rsqrt(var + GN_EPS) * gain_ref[:, sl]
        o_ref[:, sl] = (jax.nn.silu(g_ref[:, sl].astype(F32)) * o).astype(o_ref.dtype)


def retention(proj, cos_t, sin_t, gain, batch, seq, width, col0):
    tc = min(RET_CHUNK, seq)
    nc = seq // tc
    d = RET_HEAD_DIM

    def col_block(c):
        return pl.BlockSpec((tc, width), lambda b, i: (b * nc + i, c))

    return pl.pallas_call(
        _retention_kernel,
        grid=(batch, nc),
        in_specs=[col_block(col0), col_block(col0 + 1), col_block(col0 + 2), col_block(col0 + 3),
                  pl.BlockSpec((tc, d), lambda b, i: (i, 0)),
                  pl.BlockSpec((tc, d), lambda b, i: (i, 0)),
                  pl.BlockSpec((1, width), lambda b, i: (0, 0))],
        out_specs=pl.BlockSpec((tc, width), lambda b, i: (b * nc + i, 0)),
        out_shape=jax.ShapeDtypeStruct((batch * seq, width), BF16),
        scratch_shapes=[pltpu.VMEM((width // d, d, d), F32)],
        compiler_params=_params("parallel", "arbitrary"),
        name="retention",
    )(proj, proj, proj, proj, cos_t, sin_t, gain.reshape(1, width))


def _out_proj_kernel(s5_ref, ret_ref, gs_ref, w_ref, x_ref, o_ref, a_ref):
    @pl.when(pl.program_id(1) == 0)
    def _():
        half = s5_ref.shape[1]
        a_ref[:, :half] = (_rms_scale(s5_ref[...].astype(F32)) * gs_ref[...]).astype(BF16)
        a_ref[:, half:] = ret_ref[...]

    o_ref[...] = x_ref[...] + jnp.dot(a_ref[...], w_ref[...], preferred_element_type=F32)


def out_proj(y_s5, y_ret, s5_gain, w, layer, x):
    m, d_s5 = y_s5.shape
    d_ret = y_ret.shape[1]
    _, k, n = w.shape
    tm, tn = min(ROW_TILE, m), min(COL_TILE, n)
    return pl.pallas_call(
        _out_proj_kernel,
        grid=(m // tm, n // tn),
        in_specs=[pl.BlockSpec((tm, d_s5), lambda i, j: (i, 0)),
                  pl.BlockSpec((tm, d_ret), lambda i, j: (i, 0)),
                  pl.BlockSpec((1, d_s5), lambda i, j: (0, 0)),
                  pl.BlockSpec((None, k, tn), lambda i, j: (layer, 0, j)),
                  pl.BlockSpec((tm, tn), lambda i, j: (i, j))],
        out_specs=pl.BlockSpec((tm, tn), lambda i, j: (i, j)),
        out_shape=jax.ShapeDtypeStruct((m, n), F32),
        scratch_shapes=[pltpu.VMEM((tm, k), BF16)],
        compiler_params=_params("parallel", "arbitrary"),
        name="out_proj",
    )(y_s5, y_ret, s5_gain.reshape(1, d_s5), w, x)


def _swiglu_accumulate(x_ref, ln_ref, wg_ref, wu_ref, wd_ref, h_ref, acc_ref, active):
    @pl.when(pl.program_id(1) == 0)
    def _():
        h_ref[...] = (_rms_scale(x_ref[...]) * ln_ref[...]).astype(BF16)
        acc_ref[...] = jnp.zeros_like(acc_ref)

    @pl.when(active)
    def _():
        h = h_ref[...]
        gate = jnp.dot(h, wg_ref[...], preferred_element_type=F32)
        up = jnp.dot(h, wu_ref[...], preferred_element_type=F32)
        act = (jax.nn.silu(gate) * up).astype(BF16)
        acc_ref[...] += jnp.dot(act, wd_ref[...], preferred_element_type=F32)


def _dense_ffn_kernel(x_ref, ln_ref, wg_ref, wu_ref, wd_ref, o_ref, h_ref, acc_ref):
    _swiglu_accumulate(x_ref, ln_ref, wg_ref, wu_ref, wd_ref, h_ref, acc_ref, True)

    @pl.when(pl.program_id(1) == pl.num_programs(1) - 1)
    def _():
        o_ref[...] = x_ref[...] + acc_ref[...]


def dense_ffn(x, ln, wg, wu, wd, layer):
    m, d = x.shape
    f = wg.shape[2]
    tm, tf = min(FFN_ROW_TILE, m), min(FFN_HID_TILE, f)
    return pl.pallas_call(
        _dense_ffn_kernel,
        grid=(m // tm, f // tf),
        in_specs=[pl.BlockSpec((tm, d), lambda i, j: (i, 0)),
                  pl.BlockSpec((1, d), lambda i, j: (0, 0)),
                  pl.BlockSpec((None, d, tf), lambda i, j: (layer, 0, j)),
                  pl.BlockSpec((None, d, tf), lambda i, j: (layer, 0, j)),
                  pl.BlockSpec((None, tf, d), lambda i, j: (layer, j, 0))],
        out_specs=pl.BlockSpec((tm, d), lambda i, j: (i, 0)),
        out_shape=jax.ShapeDtypeStruct((m, d), F32),
        scratch_shapes=[pltpu.VMEM((tm, d), BF16), pltpu.VMEM((tm, d), F32)],
        compiler_params=_params("parallel", "arbitrary"),
        name="dense_ffn",
    )(x, ln.reshape(1, d), wg, wu, wd)


def _row_copy(src_hbm, dst_ref, src_row, dst_row, sem):
    return pltpu.make_async_copy(src_hbm.at[pl.ds(src_row, 1)], dst_ref.at[pl.ds(dst_row, 1)], sem)


def _expert_ffn_kernel(tile_expert_ref, tile_rows_ref, src_ref, x_hbm, ln_ref,
                       wg_ref, wu_ref, wd_ref, ys_hbm, xbuf, h_ref, acc_ref, gather_sem, out_sem):
    del tile_expert_ref
    i, j = pl.program_id(0), pl.program_id(1)
    n_tiles, n_hid = pl.num_programs(0), pl.num_programs(1)
    tm = xbuf.shape[0]
    rows_valid = tile_rows_ref[i]

    def gather_start(tile):
        def body(r, carry):
            _row_copy(x_hbm, xbuf, src_ref[tile * tm + r], r, gather_sem).start()
            return carry
        lax.fori_loop(0, tm, body, 0, unroll=8)

    def gather_wait():
        def body(r, carry):
            _row_copy(x_hbm, xbuf, 0, 0, gather_sem).wait()
            return carry
        lax.fori_loop(0, tm, body, 0, unroll=8)

    @pl.when(j == 0)
    def _():
        @pl.when((i == 0) & (rows_valid > 0))
        def _():
            gather_start(0)

        @pl.when(rows_valid > 0)
        def _():
            gather_wait()
            h_ref[...] = (_rms_scale(xbuf[...]) * ln_ref[...]).astype(BF16)

        acc_ref[...] = jnp.zeros_like(acc_ref)

    @pl.when((j == 1) & (i + 1 < n_tiles))
    def _():
        @pl.when(tile_rows_ref[i + 1] > 0)
        def _():
            gather_start(i + 1)

    @pl.when(rows_valid > 0)
    def _():
        wg = wg_ref[...].astype(BF16)
        wu = wu_ref[...].astype(BF16)
        wd = wd_ref[...].astype(BF16)
        for s in range(tm // EXPERT_SUB_ROWS):
            @pl.when(rows_valid > s * EXPERT_SUB_ROWS)
            def _():
                rows = pl.ds(s * EXPERT_SUB_ROWS, EXPERT_SUB_ROWS)
                h = h_ref[rows, :]
                gate = jnp.dot(h, wg, preferred_element_type=F32)
                up = jnp.dot(h, wu, preferred_element_type=F32)
                act = (jax.nn.silu(gate) * up).astype(BF16)
                acc_ref[rows, :] += jnp.dot(act, wd, preferred_element_type=F32)

    @pl.when(j == n_hid - 1)
    def _():
        out_copy = pltpu.make_async_copy(
            acc_ref, ys_hbm.at[pl.ds(pl.multiple_of(i * tm, tm), tm)], out_sem)
        out_copy.start()
        out_copy.wait()


def expert_ffn(x, src_token, tile_expert, tile_rows, ln, wg, wu, wd, layer):
    d = x.shape[1]
    r = src_token.shape[0]
    f = wg.shape[3]
    tm, tf = EXPERT_ROW_TILE, min(EXPERT_HID_TILE, f)
    nj = f // tf
    assert nj >= 2 and r % tm == 0 and tm % EXPERT_SUB_ROWS == 0

    def hid(j, tr, i):
        return jnp.where(tr[i] > 0, j, nj - 1)

    grid_spec = pltpu.PrefetchScalarGridSpec(
        num_scalar_prefetch=3,
        grid=(r // tm, nj),
        in_specs=[pl.BlockSpec(memory_space=pl.ANY),
                  pl.BlockSpec((1, d), lambda i, j, te, tr, st: (0, 0)),
                  pl.BlockSpec((None, None, d, tf),
                               lambda i, j, te, tr, st: (layer, te[i], 0, hid(j, tr, i))),
                  pl.BlockSpec((None, None, d, tf),
                               lambda i, j, te, tr, st: (layer, te[i], 0, hid(j, tr, i))),
                  pl.BlockSpec((None, None, tf, d),
                               lambda i, j, te, tr, st: (layer, te[i], hid(j, tr, i), 0))],
        out_specs=pl.BlockSpec(memory_space=pl.ANY),
        scratch_shapes=[pltpu.VMEM((tm, d), F32), pltpu.VMEM((tm, d), BF16),
                        pltpu.VMEM((tm, d), F32),
                        pltpu.SemaphoreType.DMA(()), pltpu.SemaphoreType.DMA(())])
    return pl.pallas_call(
        _expert_ffn_kernel,
        grid_spec=grid_spec,
        out_shape=jax.ShapeDtypeStruct((r, d), F32),
        compiler_params=_params("arbitrary", "arbitrary"),
        name="expert_ffn",
    )(tile_expert, tile_rows, src_token, x, ln.reshape(1, d), wg, wu, wd)


def _router_kernel(x_ref, ln_ref, w_ref, b_ref, gates_ref, sel_ref, *, n_experts):
    h = _rms_scale(x_ref[...]) * ln_ref[...]
    logits = jnp.dot(h, w_ref[...], precision=HIGHEST, preferred_element_type=F32) + b_ref[...]
    lane = lax.broadcasted_iota(jnp.int32, logits.shape, 1)
    width = logits.shape[1]
    logits = jnp.where(lane < n_experts, logits, -jnp.inf)
    top1 = jnp.max(logits, axis=-1, keepdims=True)
    idx1 = jnp.min(jnp.where(logits == top1, lane, width), axis=-1, keepdims=True)
    rest = jnp.where(lane == idx1, -jnp.inf, logits)
    top2 = jnp.max(rest, axis=-1, keepdims=True)
    idx2 = jnp.min(jnp.where(rest == top2, lane, width), axis=-1, keepdims=True)
    e2 = jnp.exp(top2 - top1)
    denom = 1.0 + e2
    gates_ref[...] = jnp.where(lane == 0, 1.0 / denom, jnp.where(lane == 1, e2 / denom, 0.0))
    sel_ref[...] = jnp.where(lane == 0, idx1, jnp.where(lane == 1, idx2, 0))


def router(x, ln, w, b):
    m, d = x.shape
    n_experts = w.shape[1]
    lanes = 128
    w_pad = jnp.pad(w.astype(F32), ((0, 0), (0, lanes - n_experts)))
    b_pad = jnp.pad(b.astype(F32), (0, lanes - n_experts)).reshape(1, lanes)
    tm = min(ROW_TILE, m)
    return pl.pallas_call(
        functools.partial(_router_kernel, n_experts=n_experts),
        grid=(m // tm,),
        in_specs=[pl.BlockSpec((tm, d), lambda i: (i, 0)),
                  pl.BlockSpec((1, d), lambda i: (0, 0)),
                  pl.BlockSpec((d, lanes), lambda i: (0, 0)),
                  pl.BlockSpec((1, lanes), lambda i: (0, 0))],
        out_specs=[pl.BlockSpec((tm, lanes), lambda i: (i, 0))] * 2,
        out_shape=[jax.ShapeDtypeStruct((m, lanes), F32),
                   jax.ShapeDtypeStruct((m, lanes), jnp.int32)],
        compiler_params=_params("parallel"),
        name="router",
    )(x, ln.reshape(1, d), w_pad, b_pad)


def dispatch_plan(sel, n_experts, tm):
    m = sel.shape[0]
    flat_e = sel[:, :TOP_K].reshape(-1)
    onehot = (flat_e[:, None] == jnp.arange(n_experts, dtype=jnp.int32)[None, :]).astype(jnp.int32)
    csum = jnp.cumsum(onehot, axis=0)
    rank = jnp.take_along_axis(csum, flat_e[:, None], axis=1)[:, 0] - 1
    counts = csum[-1]
    padded = ((counts + tm - 1) // tm) * tm
    ends = jnp.cumsum(padded)
    starts = ends - padded
    dest = (starts[flat_e] + rank).astype(jnp.int32)
    r = TOP_K * m + n_experts * tm
    src_token = jnp.zeros((r,), jnp.int32).at[dest].set(jnp.arange(TOP_K * m, dtype=jnp.int32) // TOP_K)
    tile_start = jnp.arange(r // tm, dtype=jnp.int32) * tm
    probe = jnp.minimum(tile_start, ends[-1] - 1)
    tile_expert = jnp.minimum(jnp.sum((ends[None, :] <= probe[:, None]).astype(jnp.int32), axis=1),
                              n_experts - 1).astype(jnp.int32)
    real_end = (starts + counts)[tile_expert]
    tile_rows = jnp.where(tile_start < ends[-1], jnp.clip(real_end - tile_start, 0, tm), 0)
    return src_token, dest, tile_expert, tile_rows.astype(jnp.int32)


def _combine_kernel(pos_ref, x_ref, w_ref, ys_hbm, gain_ref, o_ref, buf, sems, *, final_norm):
    i = pl.program_id(0)
    tc = x_ref.shape[0]

    def start(step, slot):
        def body(r, carry):
            t = step * tc + r
            for k in range(TOP_K):
                _row_copy(ys_hbm, buf.at[slot, k], pos_ref[TOP_K * t + k], r, sems.at[slot]).start()
            return carry
        lax.fori_loop(0, tc, body, 0, unroll=4)

    def wait(slot):
        def body(r, carry):
            for k in range(TOP_K):
                _row_copy(ys_hbm, buf.at[slot, k], 0, 0, sems.at[slot]).wait()
            return carry
        lax.fori_loop(0, tc, body, 0, unroll=4)

    slot = i % 2

    @pl.when(i == 0)
    def _():
        start(0, 0)

    @pl.when(i + 1 < pl.num_programs(0))
    def _():
        start(i + 1, 1 - slot)

    wait(slot)
    w = w_ref[...]
    out = x_ref[...] + (w[:, 0:1] * buf[slot, 0] + w[:, 1:2] * buf[slot, 1])
    if final_norm:
        out = _rms_scale(out) * gain_ref[...]
    o_ref[...] = out


def moe_combine(x, top_w, ys, dest, final_gain):
    m, d = x.shape
    tc = min(256, m)
    final_norm = final_gain is not None
    gain = (final_gain if final_norm else jnp.ones((d,), F32)).reshape(1, d)
    grid_spec = pltpu.PrefetchScalarGridSpec(
        num_scalar_prefetch=1,
        grid=(m // tc,),
        in_specs=[pl.BlockSpec((tc, d), lambda i, pos: (i, 0)),
                  pl.BlockSpec((tc, top_w.shape[1]), lambda i, pos: (i, 0)),
                  pl.BlockSpec(memory_space=pl.ANY),
                  pl.BlockSpec((1, d), lambda i, pos: (0, 0))],
        out_specs=pl.BlockSpec((tc, d), lambda i, pos: (i, 0)),
        scratch_shapes=[pltpu.VMEM((2, TOP_K, tc, d), F32), pltpu.SemaphoreType.DMA((2,))])
    return pl.pallas_call(
        functools.partial(_combine_kernel, final_norm=final_norm),
        grid_spec=grid_spec,
        out_shape=jax.ShapeDtypeStruct((m, d), F32),
        compiler_params=_params("arbitrary"),
        name="moe_combine",
    )(dest, x, top_w, ys, gain)


def moe_ffn(x, ln, router_w, router_b, wg, wu, wd, layer, final_gain):
    n_experts = router_w.shape[1]
    top_w, sel = router(x, ln, router_w, router_b)
    src_token, dest, tile_expert, tile_rows = dispatch_plan(sel, n_experts, EXPERT_ROW_TILE)
    ys = expert_ffn(x, src_token, tile_expert, tile_rows, ln, wg, wu, wd, layer)
    return moe_combine(x, top_w, ys, dest, final_gain)


def _final_norm_kernel(x_ref, g_ref, o_ref):
    o_ref[...] = _rms_scale(x_ref[...]) * g_ref[...]


def final_norm(x, gain):
    m, d = x.shape
    tm = min(ROW_TILE, m)
    return pl.pallas_call(
        _final_norm_kernel,
        grid=(m // tm,),
        in_specs=[pl.BlockSpec((tm, d), lambda i: (i, 0)), pl.BlockSpec((1, d), lambda i: (0, 0))],
        out_specs=pl.BlockSpec((tm, d), lambda i: (i, 0)),
        out_shape=jax.ShapeDtypeStruct((m, d), F32),
        compiler_params=_params("parallel"),
        name="final_norm",
    )(x, gain.reshape(1, d))


def kernel(x, ln_mix, ln_ffn, w_in, w_out, s5_lam_re, s5_lam_im, s5_log_dt, s5_b_re, s5_b_im,
           s5_c_re, s5_c_im, s5_d, s5_w_glu, s5_gain, ret_gain, ffn_w_gate, ffn_w_up, ffn_w_down,
           router_w, router_b, moe_w_gate, moe_w_up, moe_w_down, final_gain):
    batch, seq, d_model = x.shape
    depth = ln_mix.shape[0]
    d_s5, d_ret = s5_gain.shape[1], ret_gain.shape[1]
    groups = s5_lam_re.shape[1]
    t_len = min(S5_CHUNK, seq)
    n_chunks = seq // t_len
    tokens = batch * seq

    cos_t, sin_t = rope_tables(seq)
    w_in_b, w_out_b = w_in.astype(BF16), w_out.astype(BF16)
    ffn_wg_b, ffn_wu_b, ffn_wd_b = (ffn_w_gate.astype(BF16), ffn_w_up.astype(BF16),
                                    ffn_w_down.astype(BF16))
    moe_wg, moe_wu, moe_wd = moe_w_gate.astype(F32), moe_w_up.astype(F32), moe_w_down.astype(F32)
    xf = x.reshape(tokens, d_model).astype(F32)
    for i in range(depth):
        proj = norm_matmul(xf, ln_mix[i], w_in_b, i, BF16)
        u_t = (proj[:, :d_s5].reshape(batch, n_chunks, t_len, groups, S5_GROUP)
               .transpose(3, 1, 0, 2, 4).reshape(groups, n_chunks * batch, t_len * S5_GROUP))
        z = s5_groups(u_t, s5_lam_re[i], s5_lam_im[i], s5_log_dt[i], s5_b_re[i], s5_b_im[i],
                      s5_c_re[i], s5_c_im[i], s5_d[i], s5_w_glu[i], batch)
        y_s5 = (z.reshape(groups, n_chunks, batch, t_len, S5_GROUP)
                .transpose(2, 1, 3, 0, 4).reshape(tokens, d_s5))
        y_ret = retention(proj, cos_t, sin_t, ret_gain[i], batch, seq, d_ret, d_s5 // d_ret)
        xf = out_proj(y_s5, y_ret, s5_gain[i], w_out_b, i, xf)
        j = i // 2
        last = i == depth - 1
        if i % 2 == 0:
            xf = dense_ffn(xf, ln_ffn[i], ffn_wg_b, ffn_wu_b, ffn_wd_b, j)
            if last:
                xf = final_norm(xf, final_gain)
        else:
            xf = moe_ffn(xf, ln_ffn[i], router_w[j], router_b[j], moe_wg, moe_wu, moe_wd, j,
                         final_gain if last else None)
    return xf.reshape(batch, seq, d_model).astype(x.dtype)
```

```python
import functools
import math

import jax
import jax.numpy as jnp
from jax import lax
from jax.experimental import pallas as pl
from jax.experimental.pallas import tpu as pltpu

F32 = jnp.float32
BF16 = jnp.bfloat16
HIGHEST = lax.Precision.HIGHEST

S5_GROUP = 16
S5_STATE = 64
S5_LAM_RE_MAX = -1e-4
RET_HEAD_DIM = 128
ROPE_BASE = 10000.0
TOP_K = 2
NORM_EPS = 1e-6
GN_EPS = 1e-5

VMEM_LIMIT_BYTES = 56 * 1024 * 1024

S5_CHUNK = 64
RET_CHUNK = 256
ROW_TILE = 1024
COL_TILE = 1024
FFN_ROW_TILE = 512
FFN_HID_TILE = 512
EXPERT_ROW_TILE = 1024
EXPERT_SUB_ROWS = 512
EXPERT_HID_TILE = 512


def _params(*semantics):
    return pltpu.CompilerParams(dimension_semantics=semantics,
                                vmem_limit_bytes=VMEM_LIMIT_BYTES)


def _rms_scale(x):
    return x * lax.rsqrt(jnp.mean(x * x, axis=-1, keepdims=True) + NORM_EPS)


def _norm_matmul_kernel(x_ref, g_ref, w_ref, o_ref, h_ref):
    @pl.when(pl.program_id(1) == 0)
    def _():
        h_ref[...] = (_rms_scale(x_ref[...]) * g_ref[...]).astype(BF16)

    o_ref[...] = jnp.dot(h_ref[...], w_ref[...],
                         preferred_element_type=F32).astype(o_ref.dtype)


def norm_matmul(x, gain, w, layer, out_dtype):
    m, k = x.shape
    n = w.shape[2]
    tm, tn = min(ROW_TILE, m), min(COL_TILE, n)
    return pl.pallas_call(
        _norm_matmul_kernel,
        grid=(m // tm, n // tn),
        in_specs=[pl.BlockSpec((tm, k), lambda i, j: (i, 0)),
                  pl.BlockSpec((1, k), lambda i, j: (0, 0)),
                  pl.BlockSpec((None, k, tn), lambda i, j: (layer, 0, j))],
        out_specs=pl.BlockSpec((tm, tn), lambda i, j: (i, j)),
        out_shape=jax.ShapeDtypeStruct((m, n), out_dtype),
        scratch_shapes=[pltpu.VMEM((tm, k), BF16)],
        compiler_params=_params("parallel", "arbitrary"),
        name="norm_matmul",
    )(x, gain.reshape(1, k), w)


def _bf16_terms(a):
    hi = a.astype(BF16)
    rest = a - hi.astype(F32)
    mid = rest.astype(BF16)
    lo = (rest - mid.astype(F32)).astype(BF16)
    return hi, mid, lo


def _select_dot(a, sel):
    return sum(jnp.dot(t, sel, preferred_element_type=F32) for t in _bf16_terms(a))


def _dot_select(sel, a):
    return sum(jnp.dot(sel, t, preferred_element_type=F32) for t in _bf16_terms(a))


def _s5_kernel(u_ref, lrc_ref, lic_ref, lrr_ref, lir_ref, ldt_ref, cx_ref, cy_ref,
               btr_ref, bti_ref, dsk_ref, wglu_ref, z_ref,
               mt_ref, s_ref, xprev_ref, *, batch):
    rows, th = u_ref.shape
    hh = S5_GROUP
    t_len = th // hh
    p2 = 2 * S5_STATE
    n_chunks = rows // batch
    shift = hh.bit_length() - 1

    dt = jnp.exp(ldt_ref[...])

    lrc = jnp.minimum(lrc_ref[...], S5_LAM_RE_MAX) * dt
    lic = lic_ref[...] * dt
    jv = lax.broadcasted_iota(jnp.int32, (p2, t_len), 1).astype(F32)

    ar0 = jnp.exp(lrc * jv) * jnp.cos(lic * jv)
    ai0 = jnp.exp(lrc * jv) * jnp.sin(lic * jv)

    lane_th = lax.broadcasted_iota(jnp.int32, (t_len, th), 1)
    rep = jnp.where(jnp.right_shift(lane_th, shift)
                    == lax.broadcasted_iota(jnp.int32, (t_len, th), 0), 1.0, 0.0).astype(BF16)
    til = jnp.where(jnp.bitwise_and(lax.broadcasted_iota(jnp.int32, (hh, th), 1), hh - 1)
                    == lax.broadcasted_iota(jnp.int32, (hh, th), 0), 1.0, 0.0).astype(BF16)

    row_p = lax.broadcasted_iota(jnp.int32, (p2, 1), 0)
    top = row_p < S5_STATE
    cx = _select_dot(cx_ref[...], til)
    cy = _select_dot(cy_ref[...], til)
    cy = jnp.where(top, -cy, cy)

    ar0_e = _select_dot(ar0, rep)
    ai0_e = _select_dot(ai0, rep)
    a1_re = jnp.exp(lrc) * jnp.cos(lic)
    a1_im = jnp.exp(lrc) * jnp.sin(lic)
    ar1_e = ar0_e * a1_re - ai0_e * a1_im
    ai1_e = ar0_e * a1_im + ai0_e * a1_re
    ca0 = ar0_e * cx + ai0_e * cy
    ca1 = ar1_e * cx + ai1_e * cy
    sout_t = jnp.where(top, ca1, -ca1).astype(BF16)

    lrr = jnp.minimum(lrr_ref[...], S5_LAM_RE_MAX)
    lir = lir_ref[...]
    mag = jnp.exp(lrr * dt)
    ab_re = mag * jnp.cos(lir * dt)
    ab_im = mag * jnp.sin(lir * dt)
    num_re = ab_re - 1.0
    den = lrr * lrr + lir * lir
    f_re = (num_re * lrr + ab_im * lir) / den
    f_im = (ab_im * lrr - num_re * lir) / den
    b_re = btr_ref[...]
    b_im = bti_ref[...]
    bb_re = f_re * b_re - f_im * b_im
    bb_im = f_re * b_im + f_im * b_re
    left = lax.broadcasted_iota(jnp.int32, (1, p2), 1) < S5_STATE

    kt = jnp.dot(jnp.where(left, bb_re, -bb_im), ca0,
                 precision=HIGHEST, preferred_element_type=F32)

    ktpad = jnp.concatenate([jnp.zeros_like(kt), kt], axis=1)
    per_vreg = 128 // hh
    rolled = [ktpad] + [pltpu.roll(ktpad, hh * r, axis=1) for r in range(1, per_vreg)]
    for s in range(t_len):
        q, r = divmod(s, per_vreg)
        mt_ref[pl.ds(s * hh, hh), :] = rolled[r][:, th - 128 * q: 2 * th - 128 * q].astype(BF16)

    sv = lax.broadcasted_iota(jnp.int32, (t_len, p2), 0).astype(F32)
    e_in = (t_len - 1.0) - sv
    mag_s = jnp.exp(lrr * dt * e_in)
    ang_s = lir * dt * e_in
    ars = mag_s * jnp.cos(ang_s)
    ais = mag_s * jnp.sin(ang_s)
    row_th = lax.broadcasted_iota(jnp.int32, (th, t_len), 0)
    rep_t = jnp.where(jnp.right_shift(row_th, shift)
                      == lax.broadcasted_iota(jnp.int32, (th, t_len), 1), 1.0, 0.0).astype(BF16)
    a_exp = _dot_select(rep_t, jnp.concatenate([ars, ais], axis=1))
    b_sel = jnp.concatenate([jnp.where(left, bb_re, bb_im), jnp.where(left, -bb_im, bb_re),
                             jnp.where(left, bb_im, bb_re), jnp.where(left, bb_re, -bb_im)],
                            axis=1)
    b_exp = jnp.tile(b_sel, (t_len, 1))
    ars_e, ais_e = a_exp[:, :p2], a_exp[:, p2:]
    sin_p = ars_e * b_exp[:, 0 * p2:1 * p2] + ais_e * b_exp[:, 1 * p2:2 * p2]
    sin_q = ars_e * b_exp[:, 2 * p2:3 * p2] + ais_e * b_exp[:, 3 * p2:4 * p2]
    sin_pq = jnp.concatenate([sin_p, sin_q], axis=1).astype(BF16)

    x = u_ref[...]
    y = jnp.dot(x, mt_ref[...], preferred_element_type=F32)
    s_ref[...] = jnp.dot(x, sin_pq, preferred_element_type=F32)

    at_mag = jnp.exp(lrr * dt * float(t_len))
    at_re = at_mag * jnp.cos(lir * dt * float(t_len))
    at_im = at_mag * jnp.sin(lir * dt * float(t_len))
    at_im_p = jnp.where(left, -at_im, at_im)
    st_p = jnp.zeros((batch, p2), F32)
    st_q = jnp.zeros((batch, p2), F32)
    for c in range(n_chunks):
        xprev_ref[pl.ds(c * batch, batch), :] = st_p
        inj_p = s_ref[pl.ds(c * batch, batch), 0:p2]
        inj_q = s_ref[pl.ds(c * batch, batch), p2:2 * p2]
        st_p, st_q = (at_re * st_p + at_im_p * st_q + inj_p,
                      at_re * st_q - at_im_p * st_p + inj_q)

    y = y + jnp.dot(xprev_ref[...].astype(BF16), sout_t, preferred_element_type=F32)
    y = y + dsk_ref[...] * x.astype(F32)

    yg = jax.nn.gelu(y, approximate=True)
    lanes = 128
    w_lane = _select_dot(wglu_ref[...], til[:, :lanes])
    w_blk = jnp.tile(w_lane, (per_vreg, 1))
    same_step = (jnp.right_shift(lax.broadcasted_iota(jnp.int32, (lanes, lanes), 0), shift)
                 == jnp.right_shift(lax.broadcasted_iota(jnp.int32, (lanes, lanes), 1), shift))
    w_blk = jnp.where(same_step, w_blk, 0.0).astype(BF16)
    yg_b = yg.astype(BF16)
    gate = jnp.concatenate(
        [jnp.dot(yg_b[:, q * lanes:(q + 1) * lanes], w_blk, preferred_element_type=F32)
         for q in range(th // lanes)], axis=1)
    z_ref[...] = (yg * jax.nn.sigmoid(gate)).astype(z_ref.dtype)


def s5_layout_params(lam_re, lam_im, log_dt, b_re, b_im, c_re, c_im, d_skip, w_glu, t_len):
    dup = lambda a, axis: jnp.concatenate([a, a], axis=axis)
    lrc = dup(lam_re, 2)[..., None]
    lic = dup(lam_im, 2)[..., None]
    lrr = dup(lam_re, 2)[:, :, None, :]
    lir = dup(lam_im, 2)[:, :, None, :]
    ldt = log_dt[:, :, None, None]
    c_re_t = jnp.swapaxes(c_re, 2, 3)
    c_im_t = jnp.swapaxes(c_im, 2, 3)
    cx = jnp.concatenate([c_re_t, c_im_t], axis=2)
    cy = jnp.concatenate([c_im_t, c_re_t], axis=2)
    btr = dup(jnp.swapaxes(b_re, 2, 3), 3)
    bti = dup(jnp.swapaxes(b_im, 2, 3), 3)
    dsk = jnp.tile(d_skip, (1, 1, t_len))[:, :, None, :]
    return tuple(a.astype(F32) for a in (lrc, lic, lrr, lir, ldt, cx, cy, btr, bti, dsk, w_glu))


def s5_groups(u_t, params, layer, batch):
    groups, rows, th = u_t.shape
    p2 = 2 * S5_STATE

    def per_group(*tail):
        return pl.BlockSpec((None,) + tail, lambda g: (g,) + (0,) * len(tail))

    def per_layer_group(a):
        tail = a.shape[2:]
        return pl.BlockSpec((None, None) + tail, lambda g: (layer, g) + (0,) * len(tail))

    return pl.pallas_call(
        functools.partial(_s5_kernel, batch=batch),
        grid=(groups,),
        in_specs=[per_group(rows, th)] + [per_layer_group(a) for a in params],
        out_specs=per_group(rows, th),
        out_shape=jax.ShapeDtypeStruct((groups, rows, th), BF16),
        scratch_shapes=[pltpu.VMEM((th, th), BF16),
                        pltpu.VMEM((rows, 2 * p2), F32),
                        pltpu.VMEM((rows, p2), F32)],
        compiler_params=_params("parallel"),
        name="s5_groups",
    )(u_t, *params)


def _rope_table_kernel(inv_ref, cos_ref, sin_ref):
    rows, width = cos_ref.shape
    pos = (pl.program_id(0) * rows
           + lax.broadcasted_iota(jnp.int32, (rows, width), 0)).astype(F32)
    ang = pos * inv_ref[...]
    first_half = lax.broadcasted_iota(jnp.int32, (rows, width), 1) < width // 2
    cos_ref[...] = jnp.cos(ang)
    sin_ref[...] = jnp.where(first_half, -jnp.sin(ang), jnp.sin(ang))


def rope_tables(seq):
    d = RET_HEAD_DIM
    inv_freq = ROPE_BASE ** (-jnp.arange(0, d, 2, dtype=F32) / d)
    inv2 = jnp.concatenate([inv_freq, inv_freq]).reshape(1, d)
    rows = min(seq, 512)
    return pl.pallas_call(
        _rope_table_kernel,
        grid=(seq // rows,),
        in_specs=[pl.BlockSpec((1, d), lambda i: (0, 0))],
        out_specs=[pl.BlockSpec((rows, d), lambda i: (i, 0))] * 2,
        out_shape=[jax.ShapeDtypeStruct((seq, d), F32)] * 2,
        compiler_params=_params("parallel"),
        name="rope_tables",
    )(inv2)


def _retention_kernel(q_ref, k_ref, v_ref, g_ref, cos_ref, sin_ref, gain_ref, o_ref,
                      state_ref, mask_ref):
    tc, width = q_ref.shape
    d = RET_HEAD_DIM
    heads = width // d
    log_gammas = [math.log(1.0 - 2.0 ** (-5.0 - h)) for h in range(heads)]

    @pl.when((pl.program_id(0) == 0) & (pl.program_id(1) == 0))
    def _():
        diff = (lax.broadcasted_iota(jnp.int32, (tc, tc), 0)
                - lax.broadcasted_iota(jnp.int32, (tc, tc), 1))
        causal = diff >= 0
        diff_f = jnp.where(causal, diff, 0).astype(F32)
        for h in range(heads):
            mask_ref[h] = jnp.where(causal, jnp.exp(diff_f * log_gammas[h]), 0.0)

    @pl.when(pl.program_id(1) == 0)
    def _():
        state_ref[...] = jnp.zeros_like(state_ref)

    cos = cos_ref[...]
    sin = sin_ref[...]
    n_col = lax.broadcasted_iota(jnp.int32, (tc, 1), 0).astype(F32)

    def rope(t):
        return t * cos + pltpu.roll(t, d // 2, axis=1) * sin

    for h in range(heads):
        log_g = log_gammas[h]
        sl = slice(h * d, (h + 1) * d)
        q = rope(q_ref[:, sl].astype(F32))
        k = rope(k_ref[:, sl].astype(F32) * (d ** -0.5))
        v = v_ref[:, sl].astype(BF16)
        scores = lax.dot_general(q.astype(BF16), k.astype(BF16), (((1,), (1,)), ((), ())),
                                 preferred_element_type=F32) * mask_ref[h]
        inner = jnp.dot(scores.astype(BF16), v, preferred_element_type=F32)
        state = state_ref[h]
        q_dec = q * jnp.exp((n_col + 1.0) * log_g)
        cross = jnp.dot(q_dec.astype(BF16), state.astype(BF16), preferred_element_type=F32)
        k_dec = k * jnp.exp((tc - 1.0 - n_col) * log_g)
        kv = lax.dot_general(k_dec.astype(BF16), v, (((0,), (0,)), ((), ())),
                             preferred_element_type=F32)
        state_ref[h] = math.exp(tc * log_g) * state + kv
        o = inner + cross
        mu = jnp.mean(o, axis=-1, keepdims=True)
        var = jnp.mean(jnp.square(o - mu), axis=-1, keepdims=True)
        o = (o - mu) * lax.rsqrt(var + GN_EPS) * gain_ref[:, sl]
        o_ref[:, sl] = (jax.nn.silu(g_ref[:, sl].astype(F32)) * o).astype(o_ref.dtype)


def retention(proj, cos_t, sin_t, gain, batch, seq, width, col0):
    tc = min(RET_CHUNK, seq)
    nc = seq // tc
    d = RET_HEAD_DIM

    def col_block(c):
        return pl.BlockSpec((tc, width), lambda b, i: (b * nc + i, c))

    return pl.pallas_call(
        _retention_kernel,
        grid=(batch, nc),
        in_specs=[col_block(col0), col_block(col0 + 1), col_block(col0 + 2), col_block(col0 + 3),
                  pl.BlockSpec((tc, d), lambda b, i: (i, 0)),
                  pl.BlockSpec((tc, d), lambda b, i: (i, 0)),
                  pl.BlockSpec((1, width), lambda b, i: (0, 0))],
        out_specs=pl.BlockSpec((tc, width), lambda b, i: (b * nc + i, 0)),
        out_shape=jax.ShapeDtypeStruct((batch * seq, width), BF16),
        scratch_shapes=[pltpu.VMEM((width // d, d, d), F32),
                        pltpu.VMEM((width // d, tc, tc), F32)],
        compiler_params=_params("arbitrary", "arbitrary"),
        name="retention",
    )(proj, proj, proj, proj, cos_t, sin_t, gain.reshape(1, width))


def _out_proj_kernel(s5_ref, ret_ref, gs_ref, w_ref, x_ref, o_ref, a_ref):
    @pl.when(pl.program_id(1) == 0)
    def _():
        half = s5_ref.shape[1]
        a_ref[:, :half] = (_rms_scale(s5_ref[...].astype(F32)) * gs_ref[...]).astype(BF16)
        a_ref[:, half:] = ret_ref[...]

    o_ref[...] = x_ref[...] + jnp.dot(a_ref[...], w_ref[...], preferred_element_type=F32)


def out_proj(y_s5, y_ret, s5_gain, w, layer, x):
    m, d_s5 = y_s5.shape
    d_ret = y_ret.shape[1]
    _, k, n = w.shape
    tm, tn = min(ROW_TILE, m), min(COL_TILE, n)
    return pl.pallas_call(
        _out_proj_kernel,
        grid=(m // tm, n // tn),
        in_specs=[pl.BlockSpec((tm, d_s5), lambda i, j: (i, 0)),
                  pl.BlockSpec((tm, d_ret), lambda i, j: (i, 0)),
                  pl.BlockSpec((1, d_s5), lambda i, j: (0, 0)),
                  pl.BlockSpec((None, k, tn), lambda i, j: (layer, 0, j)),
                  pl.BlockSpec((tm, tn), lambda i, j: (i, j))],
        out_specs=pl.BlockSpec((tm, tn), lambda i, j: (i, j)),
        out_shape=jax.ShapeDtypeStruct((m, n), F32),
        scratch_shapes=[pltpu.VMEM((tm, k), BF16)],
        compiler_params=_params("parallel", "arbitrary"),
        name="out_proj",
    )(y_s5, y_ret, s5_gain.reshape(1, d_s5), w, x)


def _swiglu_accumulate(x_ref, ln_ref, wg_ref, wu_ref, wd_ref, h_ref, acc_ref, active):
    @pl.when(pl.program_id(1) == 0)
    def _():
        h_ref[...] = (_rms_scale(x_ref[...]) * ln_ref[...]).astype(BF16)
        acc_ref[...] = jnp.zeros_like(acc_ref)

    @pl.when(active)
    def _():
        h = h_ref[...]
        gate = jnp.dot(h, wg_ref[...], preferred_element_type=F32)
        up = jnp.dot(h, wu_ref[...], preferred_element_type=F32)
        act = (jax.nn.silu(gate) * up).astype(BF16)
        acc_ref[...] += jnp.dot(act, wd_ref[...], preferred_element_type=F32)


def _dense_ffn_kernel(x_ref, ln_ref, wg_ref, wu_ref, wd_ref, o_ref, h_ref, acc_ref):
    _swiglu_accumulate(x_ref, ln_ref, wg_ref, wu_ref, wd_ref, h_ref, acc_ref, True)

    @pl.when(pl.program_id(1) == pl.num_programs(1) - 1)
    def _():
        o_ref[...] = x_ref[...] + acc_ref[...]


def dense_ffn(x, ln, wg, wu, wd, layer):
    m, d = x.shape
    f = wg.shape[2]
    tm, tf = min(FFN_ROW_TILE, m), min(FFN_HID_TILE, f)
    return pl.pallas_call(
        _dense_ffn_kernel,
        grid=(m // tm, f // tf),
        in_specs=[pl.BlockSpec((tm, d), lambda i, j: (i, 0)),
                  pl.BlockSpec((1, d), lambda i, j: (0, 0)),
                  pl.BlockSpec((None, d, tf), lambda i, j: (layer, 0, j)),
                  pl.BlockSpec((None, d, tf), lambda i, j: (layer, 0, j)),
                  pl.BlockSpec((None, tf, d), lambda i, j: (layer, j, 0))],
        out_specs=pl.BlockSpec((tm, d), lambda i, j: (i, 0)),
        out_shape=jax.ShapeDtypeStruct((m, d), F32),
        scratch_shapes=[pltpu.VMEM((tm, d), BF16), pltpu.VMEM((tm, d), F32)],
        compiler_params=_params("parallel", "arbitrary"),
        name="dense_ffn",
    )(x, ln.reshape(1, d), wg, wu, wd)


def _row_copy(src_hbm, dst_ref, src_row, dst_row, sem):
    return pltpu.make_async_copy(src_hbm.at[pl.ds(src_row, 1)], dst_ref.at[pl.ds(dst_row, 1)], sem)


def _expert_ffn_kernel(tile_expert_ref, tile_rows_ref, src_ref, x_hbm, ln_ref,
                       wg_ref, wu_ref, wd_ref, ys_hbm, buf, h_ref, gather_sem, out_sem):
    del tile_expert_ref
    i, j = pl.program_id(0), pl.program_id(1)
    n_tiles, n_hid = pl.num_programs(0), pl.num_programs(1)
    tm = buf.shape[1]
    rows_valid = tile_rows_ref[i]
    slot = i % 2

    def gather_start(tile, dst_slot):
        def body(r, carry):
            _row_copy(x_hbm, buf.at[dst_slot], src_ref[tile * tm + r], r, gather_sem).start()
            return carry
        lax.fori_loop(0, tm, body, 0, unroll=8)

    def gather_wait():
        def body(r, carry):
            _row_copy(x_hbm, buf.at[slot], 0, 0, gather_sem).wait()
            return carry
        lax.fori_loop(0, tm, body, 0, unroll=8)

    def out_copy(tile, src_slot):
        return pltpu.make_async_copy(
            buf.at[src_slot], ys_hbm.at[pl.ds(pl.multiple_of(tile * tm, tm), tm)], out_sem)

    @pl.when(j == 0)
    def _():
        @pl.when((i == 0) & (rows_valid > 0))
        def _():
            gather_start(0, 0)

        @pl.when(rows_valid > 0)
        def _():
            gather_wait()
            h_ref[...] = (_rms_scale(buf[slot]) * ln_ref[...]).astype(BF16)

    @pl.when(j == 1)
    def _():
        @pl.when(i > 0)
        def _():
            out_copy(i - 1, 1 - slot).wait()

        nxt = jnp.minimum(i + 1, n_tiles - 1)

        @pl.when((i + 1 < n_tiles) & (tile_rows_ref[nxt] > 0))
        def _():
            gather_start(i + 1, 1 - slot)

    @pl.when(rows_valid > 0)
    def _():
        wg = wg_ref[...].astype(BF16)
        wu = wu_ref[...].astype(BF16)
        wd = wd_ref[...].astype(BF16)
        for s in range(tm // EXPERT_SUB_ROWS):
            @pl.when(rows_valid > s * EXPERT_SUB_ROWS)
            def _():
                rows = pl.ds(s * EXPERT_SUB_ROWS, EXPERT_SUB_ROWS)
                h = h_ref[rows, :]
                gate = jnp.dot(h, wg, preferred_element_type=F32)
                up = jnp.dot(h, wu, preferred_element_type=F32)
                act = (jax.nn.silu(gate) * up).astype(BF16)

                @pl.when(j == 0)
                def _():
                    buf[slot, rows, :] = jnp.dot(act, wd, preferred_element_type=F32)

                @pl.when(j > 0)
                def _():
                    buf[slot, rows, :] += jnp.dot(act, wd, preferred_element_type=F32)

    @pl.when(j == 0)
    def _():
        for s in range(tm // EXPERT_SUB_ROWS):
            @pl.when(rows_valid <= s * EXPERT_SUB_ROWS)
            def _():
                buf[slot, pl.ds(s * EXPERT_SUB_ROWS, EXPERT_SUB_ROWS), :] = jnp.zeros(
                    (EXPERT_SUB_ROWS, buf.shape[2]), F32)

    @pl.when(j == n_hid - 1)
    def _():
        out_copy(i, slot).start()

        @pl.when(i == n_tiles - 1)
        def _():
            out_copy(i, slot).wait()


def expert_ffn(x, src_token, tile_expert, tile_rows, ln, wg, wu, wd, layer):
    d = x.shape[1]
    r = src_token.shape[0]
    f = wg.shape[3]
    tm, tf = EXPERT_ROW_TILE, min(EXPERT_HID_TILE, f)
    nj = f // tf
    assert nj >= 2 and r % tm == 0 and tm % EXPERT_SUB_ROWS == 0

    def hid(j, tr, i):
        return jnp.where(tr[i] > 0, j, nj - 1)

    grid_spec = pltpu.PrefetchScalarGridSpec(
        num_scalar_prefetch=3,
        grid=(r // tm, nj),
        in_specs=[pl.BlockSpec(memory_space=pl.ANY),
                  pl.BlockSpec((1, d), lambda i, j, te, tr, st: (0, 0)),
                  pl.BlockSpec((None, None, d, tf),
                               lambda i, j, te, tr, st: (layer, te[i], 0, hid(j, tr, i))),
                  pl.BlockSpec((None, None, d, tf),
                               lambda i, j, te, tr, st: (layer, te[i], 0, hid(j, tr, i))),
                  pl.BlockSpec((None, None, tf, d),
                               lambda i, j, te, tr, st: (layer, te[i], hid(j, tr, i), 0))],
        out_specs=pl.BlockSpec(memory_space=pl.ANY),
        scratch_shapes=[pltpu.VMEM((2, tm, d), F32), pltpu.VMEM((tm, d), BF16),
                        pltpu.SemaphoreType.DMA(()), pltpu.SemaphoreType.DMA(())])
    return pl.pallas_call(
        _expert_ffn_kernel,
        grid_spec=grid_spec,
        out_shape=jax.ShapeDtypeStruct((r, d), F32),
        compiler_params=_params("arbitrary", "arbitrary"),
        name="expert_ffn",
    )(tile_expert, tile_rows, src_token, x, ln.reshape(1, d), wg, wu, wd)


def _router_kernel(x_ref, ln_ref, w_ref, b_ref, gates_ref, sel_ref, *, n_experts):
    h = _rms_scale(x_ref[...]) * ln_ref[...]
    logits = jnp.dot(h, w_ref[...], precision=HIGHEST, preferred_element_type=F32) + b_ref[...]
    lane = lax.broadcasted_iota(jnp.int32, logits.shape, 1)
    width = logits.shape[1]
    logits = jnp.where(lane < n_experts, logits, -jnp.inf)
    top1 = jnp.max(logits, axis=-1, keepdims=True)
    idx1 = jnp.min(jnp.where(logits == top1, lane, width), axis=-1, keepdims=True)
    rest = jnp.where(lane == idx1, -jnp.inf, logits)
    top2 = jnp.max(rest, axis=-1, keepdims=True)
    idx2 = jnp.min(jnp.where(rest == top2, lane, width), axis=-1, keepdims=True)
    e2 = jnp.exp(top2 - top1)
    denom = 1.0 + e2
    gates_ref[...] = jnp.where(lane == 0, 1.0 / denom, jnp.where(lane == 1, e2 / denom, 0.0))
    sel_ref[...] = jnp.where(lane == 0, idx1, jnp.where(lane == 1, idx2, 0))


def router(x, ln, w, b):
    m, d = x.shape
    n_experts = w.shape[1]
    lanes = 128
    w_pad = jnp.pad(w.astype(F32), ((0, 0), (0, lanes - n_experts)))
    b_pad = jnp.pad(b.astype(F32), (0, lanes - n_experts)).reshape(1, lanes)
    tm = min(ROW_TILE, m)
    return pl.pallas_call(
        functools.partial(_router_kernel, n_experts=n_experts),
        grid=(m // tm,),
        in_specs=[pl.BlockSpec((tm, d), lambda i: (i, 0)),
                  pl.BlockSpec((1, d), lambda i: (0, 0)),
                  pl.BlockSpec((d, lanes), lambda i: (0, 0)),
                  pl.BlockSpec((1, lanes), lambda i: (0, 0))],
        out_specs=[pl.BlockSpec((tm, lanes), lambda i: (i, 0))] * 2,
        out_shape=[jax.ShapeDtypeStruct((m, lanes), F32),
                   jax.ShapeDtypeStruct((m, lanes), jnp.int32)],
        compiler_params=_params("parallel"),
        name="router",
    )(x, ln.reshape(1, d), w_pad, b_pad)


def dispatch_plan(sel, n_experts, tm):
    m = sel.shape[0]
    flat_e = sel[:, :TOP_K].reshape(-1)
    onehot = (flat_e[:, None] == jnp.arange(n_experts, dtype=jnp.int32)[None, :]).astype(jnp.int32)
    csum = jnp.cumsum(onehot, axis=0)
    rank = jnp.take_along_axis(csum, flat_e[:, None], axis=1)[:, 0] - 1
    counts = csum[-1]
    padded = ((counts + tm - 1) // tm) * tm
    ends = jnp.cumsum(padded)
    starts = ends - padded
    dest = (starts[flat_e] + rank).astype(jnp.int32)
    r = TOP_K * m + n_experts * tm
    src_token = jnp.zeros((r,), jnp.int32).at[dest].set(jnp.arange(TOP_K * m, dtype=jnp.int32) // TOP_K)
    tile_start = jnp.arange(r // tm, dtype=jnp.int32) * tm
    probe = jnp.minimum(tile_start, ends[-1] - 1)
    tile_expert = jnp.minimum(jnp.sum((ends[None, :] <= probe[:, None]).astype(jnp.int32), axis=1),
                              n_experts - 1).astype(jnp.int32)
    real_end = (starts + counts)[tile_expert]
    tile_rows = jnp.where(tile_start < ends[-1], jnp.clip(real_end - tile_start, 0, tm), 0)
    return src_token, dest, tile_expert, tile_rows.astype(jnp.int32)


def _combine_kernel(pos_ref, x_ref, w_ref, ys_hbm, gain_ref, o_ref, buf, sems, *, final_norm):
    i = pl.program_id(0)
    tc = x_ref.shape[0]

    def start(step, slot):
        def body(r, carry):
            t = step * tc + r
            for k in range(TOP_K):
                _row_copy(ys_hbm, buf.at[slot, k], pos_ref[TOP_K * t + k], r, sems.at[slot]).start()
            return carry
        lax.fori_loop(0, tc, body, 0, unroll=4)

    def wait(slot):
        def body(r, carry):
            for k in range(TOP_K):
                _row_copy(ys_hbm, buf.at[slot, k], 0, 0, sems.at[slot]).wait()
            return carry
        lax.fori_loop(0, tc, body, 0, unroll=4)

    slot = i % 2

    @pl.when(i == 0)
    def _():
        start(0, 0)

    @pl.when(i + 1 < pl.num_programs(0))
    def _():
        start(i + 1, 1 - slot)

    wait(slot)
    w = w_ref[...]
    out = x_ref[...] + (w[:, 0:1] * buf[slot, 0] + w[:, 1:2] * buf[slot, 1])
    if final_norm:
        out = _rms_scale(out) * gain_ref[...]
    o_ref[...] = out


def moe_combine(x, top_w, ys, dest, final_gain):
    m, d = x.shape
    tc = min(256, m)
    final_norm = final_gain is not None
    gain = (final_gain if final_norm else jnp.ones((d,), F32)).reshape(1, d)
    grid_spec = pltpu.PrefetchScalarGridSpec(
        num_scalar_prefetch=1,
        grid=(m // tc,),
        in_specs=[pl.BlockSpec((tc, d), lambda i, pos: (i, 0)),
                  pl.BlockSpec((tc, top_w.shape[1]), lambda i, pos: (i, 0)),
                  pl.BlockSpec(memory_space=pl.ANY),
                  pl.BlockSpec((1, d), lambda i, pos: (0, 0))],
        out_specs=pl.BlockSpec((tc, d), lambda i, pos: (i, 0)),
        scratch_shapes=[pltpu.VMEM((2, TOP_K, tc, d), F32), pltpu.SemaphoreType.DMA((2,))])
    return pl.pallas_call(
        functools.partial(_combine_kernel, final_norm=final_norm),
        grid_spec=grid_spec,
        out_shape=jax.ShapeDtypeStruct((m, d), F32),
        compiler_params=_params("arbitrary"),
        name="moe_combine",
    )(dest, x, top_w, ys, gain)


def moe_ffn(x, ln, router_w, router_b, wg, wu, wd, layer, final_gain):
    n_experts = router_w.shape[1]
    top_w, sel = router(x, ln, router_w, router_b)
    src_token, dest, tile_expert, tile_rows = dispatch_plan(sel, n_experts, EXPERT_ROW_TILE)
    ys = expert_ffn(x, src_token, tile_expert, tile_rows, ln, wg, wu, wd, layer)
    return moe_combine(x, top_w, ys, dest, final_gain)


def _final_norm_kernel(x_ref, g_ref, o_ref):
    o_ref[...] = _rms_scale(x_ref[...]) * g_ref[...]


def final_norm(x, gain):
    m, d = x.shape
    tm = min(ROW_TILE, m)
    return pl.pallas_call(
        _final_norm_kernel,
        grid=(m // tm,),
        in_specs=[pl.BlockSpec((tm, d), lambda i: (i, 0)), pl.BlockSpec((1, d), lambda i: (0, 0))],
        out_specs=pl.BlockSpec((tm, d), lambda i: (i, 0)),
        out_shape=jax.ShapeDtypeStruct((m, d), F32),
        compiler_params=_params("parallel"),
        name="final_norm",
    )(x, gain.reshape(1, d))


def kernel(x, ln_mix, ln_ffn, w_in, w_out, s5_lam_re, s5_lam_im, s5_log_dt, s5_b_re, s5_b_im,
           s5_c_re, s5_c_im, s5_d, s5_w_glu, s5_gain, ret_gain, ffn_w_gate, ffn_w_up, ffn_w_down,
           router_w, router_b, moe_w_gate, moe_w_up, moe_w_down, final_gain):
    batch, seq, d_model = x.shape
    depth = ln_mix.shape[0]
    d_s5, d_ret = s5_gain.shape[1], ret_gain.shape[1]
    groups = s5_lam_re.shape[1]
    t_len = min(S5_CHUNK, seq)
    n_chunks = seq // t_len
    tokens = batch * seq

    cos_t, sin_t = rope_tables(seq)
    s5_params = s5_layout_params(s5_lam_re, s5_lam_im, s5_log_dt, s5_b_re, s5_b_im,
                                 s5_c_re, s5_c_im, s5_d, s5_w_glu, t_len)
    w_in_b, w_out_b = w_in.astype(BF16), w_out.astype(BF16)
    ffn_wg_b, ffn_wu_b, ffn_wd_b = (ffn_w_gate.astype(BF16), ffn_w_up.astype(BF16),
                                    ffn_w_down.astype(BF16))
    moe_wg, moe_wu, moe_wd = moe_w_gate.astype(F32), moe_w_up.astype(F32), moe_w_down.astype(F32)
    xf = x.reshape(tokens, d_model).astype(F32)
    for i in range(depth):
        proj = norm_matmul(xf, ln_mix[i], w_in_b, i, BF16)
        u_t = (proj[:, :d_s5].reshape(batch, n_chunks, t_len, groups, S5_GROUP)
               .transpose(3, 1, 0, 2, 4).reshape(groups, n_chunks * batch, t_len * S5_GROUP))
        z = s5_groups(u_t, s5_params, i, batch)
        y_s5 = (z.reshape(groups, n_chunks, batch, t_len, S5_GROUP)
                .transpose(2, 1, 3, 0, 4).reshape(tokens, d_s5))
        y_ret = retention(proj, cos_t, sin_t, ret_gain[i], batch, seq, d_ret, d_s5 // d_ret)
        xf = out_proj(y_s5, y_ret, s5_gain[i], w_out_b, i, xf)
        j = i // 2
        last = i == depth - 1
        if i % 2 == 0:
            xf = dense_ffn(xf, ln_ffn[i], ffn_wg_b, ffn_wu_b, ffn_wd_b, j)
            if last:
                xf = final_norm(xf, final_gain)
        else:
            xf = moe_ffn(xf, ln_ffn[i], router_w[j], router_b[j], moe_wg, moe_wu, moe_wd, j,
                         final_gain if last else None)
    return xf.reshape(batch, seq, d_model).astype(x.dtype)
```

```python
import functools
import math

import jax
import jax.numpy as jnp
from jax import lax
from jax.experimental import pallas as pl
from jax.experimental.pallas import tpu as pltpu

F32 = jnp.float32
BF16 = jnp.bfloat16
HIGHEST = lax.Precision.HIGHEST

S5_GROUP = 16
S5_STATE = 64
S5_LAM_RE_MAX = -1e-4
RET_HEAD_DIM = 128
ROPE_BASE = 10000.0
TOP_K = 2
NORM_EPS = 1e-6
GN_EPS = 1e-5

VMEM_LIMIT_BYTES = 56 * 1024 * 1024

S5_CHUNK = 64
S5_RELAYOUT_ROWS = 16
RET_CHUNK = 256
ROW_TILE = 1024
COL_TILE = 1024
FFN_ROW_TILE = 512
FFN_HID_TILE = 512
EXPERT_ROW_TILE = 1024
EXPERT_SUB_ROWS = 512
EXPERT_HID_TILE = 512


def _params(*semantics):
    return pltpu.CompilerParams(dimension_semantics=semantics,
                                vmem_limit_bytes=VMEM_LIMIT_BYTES)


def _rms_scale(x):
    return x * lax.rsqrt(jnp.mean(x * x, axis=-1, keepdims=True) + NORM_EPS)


def _norm_matmul_kernel(x_ref, g_ref, w_ref, o_ref, h_ref):
    @pl.when(pl.program_id(1) == 0)
    def _():
        h_ref[...] = (_rms_scale(x_ref[...]) * g_ref[...]).astype(BF16)

    o_ref[...] = jnp.dot(h_ref[...], w_ref[...],
                         preferred_element_type=F32).astype(o_ref.dtype)


def norm_matmul(x, gain, w, layer, out_dtype):
    m, k = x.shape
    n = w.shape[2]
    tm, tn = min(ROW_TILE, m), min(COL_TILE, n)
    return pl.pallas_call(
        _norm_matmul_kernel,
        grid=(m // tm, n // tn),
        in_specs=[pl.BlockSpec((tm, k), lambda i, j: (i, 0)),
                  pl.BlockSpec((1, k), lambda i, j: (0, 0)),
                  pl.BlockSpec((None, k, tn), lambda i, j: (layer, 0, j))],
        out_specs=pl.BlockSpec((tm, tn), lambda i, j: (i, j)),
        out_shape=jax.ShapeDtypeStruct((m, n), out_dtype),
        scratch_shapes=[pltpu.VMEM((tm, k), BF16)],
        compiler_params=_params("parallel", "arbitrary"),
        name="norm_matmul",
    )(x, gain.reshape(1, k), w)


def _bf16_terms(a):
    hi = a.astype(BF16)
    rest = a - hi.astype(F32)
    mid = rest.astype(BF16)
    lo = (rest - mid.astype(F32)).astype(BF16)
    return hi, mid, lo


def _select_dot(a, sel):
    return sum(jnp.dot(t, sel, preferred_element_type=F32) for t in _bf16_terms(a))


def _dot_select(sel, a):
    return sum(jnp.dot(sel, t, preferred_element_type=F32) for t in _bf16_terms(a))


def _s5_kernel(u_ref, lrc_ref, lic_ref, lrr_ref, lir_ref, ldt_ref, cx_ref, cy_ref,
               btr_ref, bti_ref, dsk_ref, wglu_ref, z_ref,
               mt_ref, s_ref, xprev_ref, *, batch):
    rows, th = u_ref.shape
    hh = S5_GROUP
    t_len = th // hh
    p2 = 2 * S5_STATE
    n_chunks = rows // batch
    shift = hh.bit_length() - 1

    dt = jnp.exp(ldt_ref[...])

    lrc = jnp.minimum(lrc_ref[...], S5_LAM_RE_MAX) * dt
    lic = lic_ref[...] * dt
    jv = lax.broadcasted_iota(jnp.int32, (p2, t_len), 1).astype(F32)

    ar0 = jnp.exp(lrc * jv) * jnp.cos(lic * jv)
    ai0 = jnp.exp(lrc * jv) * jnp.sin(lic * jv)

    lane_th = lax.broadcasted_iota(jnp.int32, (t_len, th), 1)
    rep = jnp.where(jnp.right_shift(lane_th, shift)
                    == lax.broadcasted_iota(jnp.int32, (t_len, th), 0), 1.0, 0.0).astype(BF16)
    til = jnp.where(jnp.bitwise_and(lax.broadcasted_iota(jnp.int32, (hh, th), 1), hh - 1)
                    == lax.broadcasted_iota(jnp.int32, (hh, th), 0), 1.0, 0.0).astype(BF16)

    row_p = lax.broadcasted_iota(jnp.int32, (p2, 1), 0)
    top = row_p < S5_STATE
    cx = _select_dot(cx_ref[...], til)
    cy = _select_dot(cy_ref[...], til)
    cy = jnp.where(top, -cy, cy)

    ar0_e = _select_dot(ar0, rep)
    ai0_e = _select_dot(ai0, rep)
    a1_re = jnp.exp(lrc) * jnp.cos(lic)
    a1_im = jnp.exp(lrc) * jnp.sin(lic)
    ar1_e = ar0_e * a1_re - ai0_e * a1_im
    ai1_e = ar0_e * a1_im + ai0_e * a1_re
    ca0 = ar0_e * cx + ai0_e * cy
    ca1 = ar1_e * cx + ai1_e * cy
    sout_t = jnp.where(top, ca1, -ca1).astype(BF16)

    lrr = jnp.minimum(lrr_ref[...], S5_LAM_RE_MAX)
    lir = lir_ref[...]
    mag = jnp.exp(lrr * dt)
    ab_re = mag * jnp.cos(lir * dt)
    ab_im = mag * jnp.sin(lir * dt)
    num_re = ab_re - 1.0
    den = lrr * lrr + lir * lir
    f_re = (num_re * lrr + ab_im * lir) / den
    f_im = (ab_im * lrr - num_re * lir) / den
    b_re = btr_ref[...]
    b_im = bti_ref[...]
    bb_re = f_re * b_re - f_im * b_im
    bb_im = f_re * b_im + f_im * b_re
    left = lax.broadcasted_iota(jnp.int32, (1, p2), 1) < S5_STATE

    kt = jnp.dot(jnp.where(left, bb_re, -bb_im), ca0,
                 precision=HIGHEST, preferred_element_type=F32)

    ktpad = jnp.concatenate([jnp.zeros_like(kt), kt], axis=1)
    per_vreg = 128 // hh
    rolled = [ktpad] + [pltpu.roll(ktpad, hh * r, axis=1) for r in range(1, per_vreg)]
    for s in range(t_len):
        q, r = divmod(s, per_vreg)
        mt_ref[pl.ds(s * hh, hh), :] = rolled[r][:, th - 128 * q: 2 * th - 128 * q].astype(BF16)

    sv = lax.broadcasted_iota(jnp.int32, (t_len, p2), 0).astype(F32)
    e_in = (t_len - 1.0) - sv
    mag_s = jnp.exp(lrr * dt * e_in)
    ang_s = lir * dt * e_in
    ars = mag_s * jnp.cos(ang_s)
    ais = mag_s * jnp.sin(ang_s)
    row_th = lax.broadcasted_iota(jnp.int32, (th, t_len), 0)
    rep_t = jnp.where(jnp.right_shift(row_th, shift)
                      == lax.broadcasted_iota(jnp.int32, (th, t_len), 1), 1.0, 0.0).astype(BF16)
    a_exp = _dot_select(rep_t, jnp.concatenate([ars, ais], axis=1))
    b_sel = jnp.concatenate([jnp.where(left, bb_re, bb_im), jnp.where(left, -bb_im, bb_re),
                             jnp.where(left, bb_im, bb_re), jnp.where(left, bb_re, -bb_im)],
                            axis=1)
    b_exp = jnp.tile(b_sel, (t_len, 1))
    ars_e, ais_e = a_exp[:, :p2], a_exp[:, p2:]
    sin_p = ars_e * b_exp[:, 0 * p2:1 * p2] + ais_e * b_exp[:, 1 * p2:2 * p2]
    sin_q = ars_e * b_exp[:, 2 * p2:3 * p2] + ais_e * b_exp[:, 3 * p2:4 * p2]
    sin_pq = jnp.concatenate([sin_p, sin_q], axis=1).astype(BF16)

    x = u_ref[...]
    y = jnp.dot(x, mt_ref[...], preferred_element_type=F32)
    inj = jnp.dot(x, sin_pq, preferred_element_type=F32)
    s_ref[0] = inj[:, 0:p2]
    s_ref[1] = inj[:, p2:2 * p2]

    at_mag = jnp.exp(lrr * dt * float(t_len))
    at_re = at_mag * jnp.cos(lir * dt * float(t_len))
    at_im = at_mag * jnp.sin(lir * dt * float(t_len))
    at_im_p = jnp.where(left, -at_im, at_im)
    st_p = jnp.zeros((batch, p2), F32)
    st_q = jnp.zeros((batch, p2), F32)
    for c in range(n_chunks):
        for b in range(batch):
            xprev_ref[pl.ds(b * n_chunks + c, 1), :] = st_p[b:b + 1]
        inj_p = s_ref[0, pl.ds(c, batch, stride=n_chunks), :]
        inj_q = s_ref[1, pl.ds(c, batch, stride=n_chunks), :]
        st_p, st_q = (at_re * st_p + at_im_p * st_q + inj_p,
                      at_re * st_q - at_im_p * st_p + inj_q)

    y = y + jnp.dot(xprev_ref[...].astype(BF16), sout_t, preferred_element_type=F32)
    y = y + dsk_ref[...] * x.astype(F32)

    yg = jax.nn.gelu(y, approximate=True)
    lanes = 128
    w_lane = _select_dot(wglu_ref[...], til[:, :lanes])
    w_blk = jnp.tile(w_lane, (per_vreg, 1))
    same_step = (jnp.right_shift(lax.broadcasted_iota(jnp.int32, (lanes, lanes), 0), shift)
                 == jnp.right_shift(lax.broadcasted_iota(jnp.int32, (lanes, lanes), 1), shift))
    w_blk = jnp.where(same_step, w_blk, 0.0).astype(BF16)
    yg_b = yg.astype(BF16)
    gate = jnp.concatenate(
        [jnp.dot(yg_b[:, q * lanes:(q + 1) * lanes], w_blk, preferred_element_type=F32)
         for q in range(th // lanes)], axis=1)
    z_ref[...] = (yg * jax.nn.sigmoid(gate)).astype(z_ref.dtype)


def _granule_transpose(w):
    n, lanes = w.shape
    assert lanes // S5_GROUP == 8 and n % 8 == 0
    w = w.reshape(n // 8, 8, lanes)
    row = lax.broadcasted_iota(jnp.int32, (1, 8, lanes), 1)
    gran = jnp.right_shift(lax.broadcasted_iota(jnp.int32, (1, 8, lanes), 2), S5_GROUP.bit_length() - 1)
    for d in (4, 2, 1):
        bit = d.bit_length() - 1
        row_bit = jnp.bitwise_and(jnp.right_shift(row, bit), 1)
        gran_bit = jnp.bitwise_and(jnp.right_shift(gran, bit), 1)
        from_above = pltpu.roll(pltpu.roll(w, d, axis=1), lanes - S5_GROUP * d, axis=2)
        from_below = pltpu.roll(pltpu.roll(w, 8 - d, axis=1), S5_GROUP * d, axis=2)
        w = jnp.where(row_bit == gran_bit, w, jnp.where(row_bit == 1, from_above, from_below))
    return w.reshape(n, lanes)


def _s5_relayout_in_kernel(u_ref, o_ref, w_ref, *, t_len):
    n_rows = o_ref.shape[1]
    piece = S5_RELAYOUT_ROWS * t_len

    def body(p, carry):
        tok0 = pl.multiple_of(p * piece, piece)
        w_ref[...] = _granule_transpose(u_ref[pl.ds(tok0, piece), :].astype(F32))
        r0 = pl.multiple_of(p * S5_RELAYOUT_ROWS, S5_RELAYOUT_ROWS)
        for g in range(o_ref.shape[0]):
            for th in range(t_len // 8):
                tile = w_ref[pl.ds(th * 8 + g, S5_RELAYOUT_ROWS, stride=t_len), :]
                o_ref[g, pl.ds(r0, S5_RELAYOUT_ROWS), th * 128:(th + 1) * 128] = tile.astype(o_ref.dtype)
        return carry

    lax.fori_loop(0, n_rows // S5_RELAYOUT_ROWS, body, 0)


def _s5_relayout_out_kernel(z_ref, o_ref, zf_ref, *, t_len):
    n_groups, n_rows, _ = z_ref.shape
    piece = S5_RELAYOUT_ROWS * t_len
    for g in range(n_groups):
        for th in range(t_len // 8):
            zf_ref[th, pl.ds(g * n_rows, n_rows), :] = z_ref[g, :, th * 128:(th + 1) * 128].astype(F32)

    def body(p, carry):
        r0 = p * S5_RELAYOUT_ROWS
        tiles = [zf_ref[th, pl.ds(r0 + rl, n_groups, stride=n_rows), :]
                 for rl in range(S5_RELAYOUT_ROWS) for th in range(t_len // 8)]
        w = jnp.concatenate(tiles, axis=0)
        tok0 = pl.multiple_of(p * piece, piece)
        o_ref[pl.ds(tok0, piece), :] = _granule_transpose(w).astype(o_ref.dtype)
        return carry

    lax.fori_loop(0, n_rows // S5_RELAYOUT_ROWS, body, 0)


def s5_relayout_in(proj, d_s5, t_len):
    tokens = proj.shape[0]
    lane_groups = 128 // S5_GROUP
    rows = tokens // t_len
    return pl.pallas_call(
        functools.partial(_s5_relayout_in_kernel, t_len=t_len),
        grid=(d_s5 // 128,),
        in_specs=[pl.BlockSpec((tokens, 128), lambda i: (0, i))],
        out_specs=pl.BlockSpec((lane_groups, rows, t_len * S5_GROUP), lambda i: (i, 0, 0)),
        out_shape=jax.ShapeDtypeStruct((d_s5 // S5_GROUP, rows, t_len * S5_GROUP), BF16),
        scratch_shapes=[pltpu.VMEM((S5_RELAYOUT_ROWS * t_len, 128), F32)],
        compiler_params=_params("parallel"),
        name="s5_relayout_in",
    )(proj)


def s5_relayout_out(z, t_len):
    groups, rows, th = z.shape
    lane_groups = 128 // S5_GROUP
    tokens = rows * t_len
    return pl.pallas_call(
        functools.partial(_s5_relayout_out_kernel, t_len=t_len),
        grid=(groups // lane_groups,),
        in_specs=[pl.BlockSpec((lane_groups, rows, th), lambda i: (i, 0, 0))],
        out_specs=pl.BlockSpec((tokens, 128), lambda i: (0, i)),
        out_shape=jax.ShapeDtypeStruct((tokens, groups * S5_GROUP), BF16),
        scratch_shapes=[pltpu.VMEM((th // 128, lane_groups * rows, 128), F32)],
        compiler_params=_params("parallel"),
        name="s5_relayout_out",
    )(z)


def s5_layout_params(lam_re, lam_im, log_dt, b_re, b_im, c_re, c_im, d_skip, w_glu, t_len):
    dup = lambda a, axis: jnp.concatenate([a, a], axis=axis)
    lrc = dup(lam_re, 2)[..., None]
    lic = dup(lam_im, 2)[..., None]
    lrr = dup(lam_re, 2)[:, :, None, :]
    lir = dup(lam_im, 2)[:, :, None, :]
    ldt = log_dt[:, :, None, None]
    c_re_t = jnp.swapaxes(c_re, 2, 3)
    c_im_t = jnp.swapaxes(c_im, 2, 3)
    cx = jnp.concatenate([c_re_t, c_im_t], axis=2)
    cy = jnp.concatenate([c_im_t, c_re_t], axis=2)
    btr = dup(jnp.swapaxes(b_re, 2, 3), 3)
    bti = dup(jnp.swapaxes(b_im, 2, 3), 3)
    dsk = jnp.tile(d_skip, (1, 1, t_len))[:, :, None, :]
    return tuple(a.astype(F32) for a in (lrc, lic, lrr, lir, ldt, cx, cy, btr, bti, dsk, w_glu))


def s5_groups(u_t, params, layer, batch):
    groups, rows, th = u_t.shape
    p2 = 2 * S5_STATE

    def per_group(*tail):
        return pl.BlockSpec((None,) + tail, lambda g: (g,) + (0,) * len(tail))

    def per_layer_group(a):
        tail = a.shape[2:]
        return pl.BlockSpec((None, None) + tail, lambda g: (layer, g) + (0,) * len(tail))

    return pl.pallas_call(
        functools.partial(_s5_kernel, batch=batch),
        grid=(groups,),
        in_specs=[per_group(rows, th)] + [per_layer_group(a) for a in params],
        out_specs=per_group(rows, th),
        out_shape=jax.ShapeDtypeStruct((groups, rows, th), BF16),
        scratch_shapes=[pltpu.VMEM((th, th), BF16),
                        pltpu.VMEM((2, rows, p2), F32),
                        pltpu.VMEM((rows, p2), F32)],
        compiler_params=_params("parallel"),
        name="s5_groups",
    )(u_t, *params)


def _rope_table_kernel(inv_ref, cos_ref, sin_ref):
    rows, width = cos_ref.shape
    pos = (pl.program_id(0) * rows
           + lax.broadcasted_iota(jnp.int32, (rows, width), 0)).astype(F32)
    ang = pos * inv_ref[...]
    first_half = lax.broadcasted_iota(jnp.int32, (rows, width), 1) < width // 2
    cos_ref[...] = jnp.cos(ang)
    sin_ref[...] = jnp.where(first_half, -jnp.sin(ang), jnp.sin(ang))


def rope_tables(seq):
    d = RET_HEAD_DIM
    inv_freq = ROPE_BASE ** (-jnp.arange(0, d, 2, dtype=F32) / d)
    inv2 = jnp.concatenate([inv_freq, inv_freq]).reshape(1, d)
    rows = min(seq, 512)
    return pl.pallas_call(
        _rope_table_kernel,
        grid=(seq // rows,),
        in_specs=[pl.BlockSpec((1, d), lambda i: (0, 0))],
        out_specs=[pl.BlockSpec((rows, d), lambda i: (i, 0))] * 2,
        out_shape=[jax.ShapeDtypeStruct((seq, d), F32)] * 2,
        compiler_params=_params("parallel"),
        name="rope_tables",
    )(inv2)


def _retention_kernel(q_ref, k_ref, v_ref, g_ref, cos_ref, sin_ref, gain_ref, o_ref,
                      state_ref, mask_ref):
    tc, width = q_ref.shape
    d = RET_HEAD_DIM
    heads = width // d
    log_gammas = [math.log(1.0 - 2.0 ** (-5.0 - h)) for h in range(heads)]

    @pl.when((pl.program_id(0) == 0) & (pl.program_id(1) == 0))
    def _():
        diff = (lax.broadcasted_iota(jnp.int32, (tc, tc), 0)
                - lax.broadcasted_iota(jnp.int32, (tc, tc), 1))
        causal = diff >= 0
        diff_f = jnp.where(causal, diff, 0).astype(F32)
        for h in range(heads):
            mask_ref[h] = jnp.where(causal, jnp.exp(diff_f * log_gammas[h]), 0.0)

    @pl.when(pl.program_id(1) == 0)
    def _():
        state_ref[...] = jnp.zeros_like(state_ref)

    cos = cos_ref[...]
    sin = sin_ref[...]
    n_col = lax.broadcasted_iota(jnp.int32, (tc, 1), 0).astype(F32)

    def rope(t):
        return t * cos + pltpu.roll(t, d // 2, axis=1) * sin

    for h in range(heads):
        log_g = log_gammas[h]
        sl = slice(h * d, (h + 1) * d)
        q = rope(q_ref[:, sl].astype(F32))
        k = rope(k_ref[:, sl].astype(F32) * (d ** -0.5))
        v = v_ref[:, sl].astype(BF16)
        scores = lax.dot_general(q.astype(BF16), k.astype(BF16), (((1,), (1,)), ((), ())),
                                 preferred_element_type=F32) * mask_ref[h]
        inner = jnp.dot(scores.astype(BF16), v, preferred_element_type=F32)
        state = state_ref[h]
        q_dec = q * jnp.exp((n_col + 1.0) * log_g)
        cross = jnp.dot(q_dec.astype(BF16), state.astype(BF16), preferred_element_type=F32)
        k_dec = k * jnp.exp((tc - 1.0 - n_col) * log_g)
        kv = lax.dot_general(k_dec.astype(BF16), v, (((0,), (0,)), ((), ())),
                             preferred_element_type=F32)
        state_ref[h] = math.exp(tc * log_g) * state + kv
        o = inner + cross
        mu = jnp.mean(o, axis=-1, keepdims=True)
        var = jnp.mean(jnp.square(o - mu), axis=-1, keepdims=True)
        o = (o - mu) * lax.rsqrt(var + GN_EPS) * gain_ref[:, sl]
        o_ref[:, sl] = (jax.nn.silu(g_ref[:, sl].astype(F32)) * o).astype(o_ref.dtype)


def retention(proj, cos_t, sin_t, gain, batch, seq, width, col0):
    tc = min(RET_CHUNK, seq)
    nc = seq // tc
    d = RET_HEAD_DIM

    def col_block(c):
        return pl.BlockSpec((tc, width), lambda b, i: (b * nc + i, c))

    return pl.pallas_call(
        _retention_kernel,
        grid=(batch, nc),
        in_specs=[col_block(col0), col_block(col0 + 1), col_block(col0 + 2), col_block(col0 + 3),
                  pl.BlockSpec((tc, d), lambda b, i: (i, 0)),
                  pl.BlockSpec((tc, d), lambda b, i: (i, 0)),
                  pl.BlockSpec((1, width), lambda b, i: (0, 0))],
        out_specs=pl.BlockSpec((tc, width), lambda b, i: (b * nc + i, 0)),
        out_shape=jax.ShapeDtypeStruct((batch * seq, width), BF16),
        scratch_shapes=[pltpu.VMEM((width // d, d, d), F32),
                        pltpu.VMEM((width // d, tc, tc), F32)],
        compiler_params=_params("arbitrary", "arbitrary"),
        name="retention",
    )(proj, proj, proj, proj, cos_t, sin_t, gain.reshape(1, width))


def _out_proj_kernel(s5_ref, ret_ref, gs_ref, w_ref, x_ref, o_ref, a_ref):
    @pl.when(pl.program_id(1) == 0)
    def _():
        half = s5_ref.shape[1]
        a_ref[:, :half] = (_rms_scale(s5_ref[...].astype(F32)) * gs_ref[...]).astype(BF16)
        a_ref[:, half:] = ret_ref[...]

    o_ref[...] = x_ref[...] + jnp.dot(a_ref[...], w_ref[...], preferred_element_type=F32)


def out_proj(y_s5, y_ret, s5_gain, w, layer, x):
    m, d_s5 = y_s5.shape
    d_ret = y_ret.shape[1]
    _, k, n = w.shape
    tm, tn = min(ROW_TILE, m), min(COL_TILE, n)
    return pl.pallas_call(
        _out_proj_kernel,
        grid=(m // tm, n // tn),
        in_specs=[pl.BlockSpec((tm, d_s5), lambda i, j: (i, 0)),
                  pl.BlockSpec((tm, d_ret), lambda i, j: (i, 0)),
                  pl.BlockSpec((1, d_s5), lambda i, j: (0, 0)),
                  pl.BlockSpec((None, k, tn), lambda i, j: (layer, 0, j)),
                  pl.BlockSpec((tm, tn), lambda i, j: (i, j))],
        out_specs=pl.BlockSpec((tm, tn), lambda i, j: (i, j)),
        out_shape=jax.ShapeDtypeStruct((m, n), F32),
        scratch_shapes=[pltpu.VMEM((tm, k), BF16)],
        compiler_params=_params("parallel", "arbitrary"),
        name="out_proj",
    )(y_s5, y_ret, s5_gain.reshape(1, d_s5), w, x)


def _swiglu_accumulate(x_ref, ln_ref, wg_ref, wu_ref, wd_ref, h_ref, acc_ref, active):
    @pl.when(pl.program_id(1) == 0)
    def _():
        h_ref[...] = (_rms_scale(x_ref[...]) * ln_ref[...]).astype(BF16)
        acc_ref[...] = jnp.zeros_like(acc_ref)

    @pl.when(active)
    def _():
        h = h_ref[...]
        gate = jnp.dot(h, wg_ref[...], preferred_element_type=F32)
        up = jnp.dot(h, wu_ref[...], preferred_element_type=F32)
        act = (jax.nn.silu(gate) * up).astype(BF16)
        acc_ref[...] += jnp.dot(act, wd_ref[...], preferred_element_type=F32)


def _dense_ffn_kernel(x_ref, ln_ref, wg_ref, wu_ref, wd_ref, o_ref, h_ref, acc_ref):
    _swiglu_accumulate(x_ref, ln_ref, wg_ref, wu_ref, wd_ref, h_ref, acc_ref, True)

    @pl.when(pl.program_id(1) == pl.num_programs(1) - 1)
    def _():
        o_ref[...] = x_ref[...] + acc_ref[...]


def dense_ffn(x, ln, wg, wu, wd, layer):
    m, d = x.shape
    f = wg.shape[2]
    tm, tf = min(FFN_ROW_TILE, m), min(FFN_HID_TILE, f)
    return pl.pallas_call(
        _dense_ffn_kernel,
        grid=(m // tm, f // tf),
        in_specs=[pl.BlockSpec((tm, d), lambda i, j: (i, 0)),
                  pl.BlockSpec((1, d), lambda i, j: (0, 0)),
                  pl.BlockSpec((None, d, tf), lambda i, j: (layer, 0, j)),
                  pl.BlockSpec((None, d, tf), lambda i, j: (layer, 0, j)),
                  pl.BlockSpec((None, tf, d), lambda i, j: (layer, j, 0))],
        out_specs=pl.BlockSpec((tm, d), lambda i, j: (i, 0)),
        out_shape=jax.ShapeDtypeStruct((m, d), F32),
        scratch_shapes=[pltpu.VMEM((tm, d), BF16), pltpu.VMEM((tm, d), F32)],
        compiler_params=_params("parallel", "arbitrary"),
        name="dense_ffn",
    )(x, ln.reshape(1, d), wg, wu, wd)


def _row_copy(src_hbm, dst_ref, src_row, dst_row, sem):
    return pltpu.make_async_copy(src_hbm.at[pl.ds(src_row, 1)], dst_ref.at[pl.ds(dst_row, 1)], sem)


def _expert_ffn_kernel(tile_expert_ref, tile_rows_ref, src_ref, x_hbm, ln_ref,
                       wg_ref, wu_ref, wd_ref, ys_hbm, xbuf, h_ref, acc_ref, gather_sem, out_sem):
    del tile_expert_ref
    i, j = pl.program_id(0), pl.program_id(1)
    n_tiles, n_hid = pl.num_programs(0), pl.num_programs(1)
    tm = xbuf.shape[0]
    rows_valid = tile_rows_ref[i]

    def gather_start(tile):
        def body(r, carry):
            _row_copy(x_hbm, xbuf, src_ref[tile * tm + r], r, gather_sem).start()
            return carry
        lax.fori_loop(0, tm, body, 0, unroll=8)

    def gather_wait():
        def body(r, carry):
            _row_copy(x_hbm, xbuf, 0, 0, gather_sem).wait()
            return carry
        lax.fori_loop(0, tm, body, 0, unroll=8)

    @pl.when(j == 0)
    def _():
        @pl.when((i == 0) & (rows_valid > 0))
        def _():
            gather_start(0)

        @pl.when(rows_valid > 0)
        def _():
            gather_wait()
            h_ref[...] = (_rms_scale(xbuf[...]) * ln_ref[...]).astype(BF16)

        acc_ref[...] = jnp.zeros_like(acc_ref)

    @pl.when((j == 1) & (i + 1 < n_tiles))
    def _():
        @pl.when(tile_rows_ref[i + 1] > 0)
        def _():
            gather_start(i + 1)

    @pl.when(rows_valid > 0)
    def _():
        wg = wg_ref[...].astype(BF16)
        wu = wu_ref[...].astype(BF16)
        wd = wd_ref[...].astype(BF16)
        for s in range(tm // EXPERT_SUB_ROWS):
            @pl.when(rows_valid > s * EXPERT_SUB_ROWS)
            def _():
                rows = pl.ds(s * EXPERT_SUB_ROWS, EXPERT_SUB_ROWS)
                h = h_ref[rows, :]
                gate = jnp.dot(h, wg, preferred_element_type=F32)
                up = jnp.dot(h, wu, preferred_element_type=F32)
                act = (jax.nn.silu(gate) * up).astype(BF16)
                acc_ref[rows, :] += jnp.dot(act, wd, preferred_element_type=F32)

    @pl.when(j == n_hid - 1)
    def _():
        out_copy = pltpu.make_async_copy(
            acc_ref, ys_hbm.at[pl.ds(pl.multiple_of(i * tm, tm), tm)], out_sem)
        out_copy.start()
        out_copy.wait()


def expert_ffn(x, src_token, tile_expert, tile_rows, ln, wg, wu, wd, layer):
    d = x.shape[1]
    r = src_token.shape[0]
    f = wg.shape[3]
    tm, tf = EXPERT_ROW_TILE, min(EXPERT_HID_TILE, f)
    nj = f // tf
    assert nj >= 2 and r % tm == 0 and tm % EXPERT_SUB_ROWS == 0

    def hid(j, tr, i):
        return jnp.where(tr[i] > 0, j, nj - 1)

    grid_spec = pltpu.PrefetchScalarGridSpec(
        num_scalar_prefetch=3,
        grid=(r // tm, nj),
        in_specs=[pl.BlockSpec(memory_space=pl.ANY),
                  pl.BlockSpec((1, d), lambda i, j, te, tr, st: (0, 0)),
                  pl.BlockSpec((None, None, d, tf),
                               lambda i, j, te, tr, st: (layer, te[i], 0, hid(j, tr, i))),
                  pl.BlockSpec((None, None, d, tf),
                               lambda i, j, te, tr, st: (layer, te[i], 0, hid(j, tr, i))),
                  pl.BlockSpec((None, None, tf, d),
                               lambda i, j, te, tr, st: (layer, te[i], hid(j, tr, i), 0))],
        out_specs=pl.BlockSpec(memory_space=pl.ANY),
        scratch_shapes=[pltpu.VMEM((tm, d), F32), pltpu.VMEM((tm, d), BF16),
                        pltpu.VMEM((tm, d), F32),
                        pltpu.SemaphoreType.DMA(()), pltpu.SemaphoreType.DMA(())])
    return pl.pallas_call(
        _expert_ffn_kernel,
        grid_spec=grid_spec,
        out_shape=jax.ShapeDtypeStruct((r, d), F32),
        compiler_params=_params("arbitrary", "arbitrary"),
        name="expert_ffn",
    )(tile_expert, tile_rows, src_token, x, ln.reshape(1, d), wg, wu, wd)


def _router_kernel(x_ref, ln_ref, w_ref, b_ref, gates_ref, sel_ref, *, n_experts):
    h = _rms_scale(x_ref[...]) * ln_ref[...]
    logits = jnp.dot(h, w_ref[...], precision=HIGHEST, preferred_element_type=F32) + b_ref[...]
    lane = lax.broadcasted_iota(jnp.int32, logits.shape, 1)
    width = logits.shape[1]
    logits = jnp.where(lane < n_experts, logits, -jnp.inf)
    top1 = jnp.max(logits, axis=-1, keepdims=True)
    idx1 = jnp.min(jnp.where(logits == top1, lane, width), axis=-1, keepdims=True)
    rest = jnp.where(lane == idx1, -jnp.inf, logits)
    top2 = jnp.max(rest, axis=-1, keepdims=True)
    idx2 = jnp.min(jnp.where(rest == top2, lane, width), axis=-1, keepdims=True)
    e2 = jnp.exp(top2 - top1)
    denom = 1.0 + e2
    gates_ref[...] = jnp.where(lane == 0, 1.0 / denom, jnp.where(lane == 1, e2 / denom, 0.0))
    sel_ref[...] = jnp.where(lane == 0, idx1, jnp.where(lane == 1, idx2, 0))


def router(x, ln, w, b):
    m, d = x.shape
    n_experts = w.shape[1]
    lanes = 128
    w_pad = jnp.pad(w.astype(F32), ((0, 0), (0, lanes - n_experts)))
    b_pad = jnp.pad(b.astype(F32), (0, lanes - n_experts)).reshape(1, lanes)
    tm = min(ROW_TILE, m)
    return pl.pallas_call(
        functools.partial(_router_kernel, n_experts=n_experts),
        grid=(m // tm,),
        in_specs=[pl.BlockSpec((tm, d), lambda i: (i, 0)),
                  pl.BlockSpec((1, d), lambda i: (0, 0)),
                  pl.BlockSpec((d, lanes), lambda i: (0, 0)),
                  pl.BlockSpec((1, lanes), lambda i: (0, 0))],
        out_specs=[pl.BlockSpec((tm, lanes), lambda i: (i, 0))] * 2,
        out_shape=[jax.ShapeDtypeStruct((m, lanes), F32),
                   jax.ShapeDtypeStruct((m, lanes), jnp.int32)],
        compiler_params=_params("parallel"),
        name="router",
    )(x, ln.reshape(1, d), w_pad, b_pad)


def dispatch_plan(sel, n_experts, tm):
    m = sel.shape[0]
    flat_e = sel[:, :TOP_K].reshape(-1)
    onehot = (flat_e[:, None] == jnp.arange(n_experts, dtype=jnp.int32)[None, :]).astype(jnp.int32)
    csum = jnp.cumsum(onehot, axis=0)
    rank = jnp.take_along_axis(csum, flat_e[:, None], axis=1)[:, 0] - 1
    counts = csum[-1]
    padded = ((counts + tm - 1) // tm) * tm
    ends = jnp.cumsum(padded)
    starts = ends - padded
    dest = (starts[flat_e] + rank).astype(jnp.int32)
    r = TOP_K * m + n_experts * tm
    src_token = jnp.zeros((r,), jnp.int32).at[dest].set(jnp.arange(TOP_K * m, dtype=jnp.int32) // TOP_K)
    tile_start = jnp.arange(r // tm, dtype=jnp.int32) * tm
    probe = jnp.minimum(tile_start, ends[-1] - 1)
    tile_expert = jnp.minimum(jnp.sum((ends[None, :] <= probe[:, None]).astype(jnp.int32), axis=1),
                              n_experts - 1).astype(jnp.int32)
    real_end = (starts + counts)[tile_expert]
    tile_rows = jnp.where(tile_start < ends[-1], jnp.clip(real_end - tile_start, 0, tm), 0)
    return src_token, dest, tile_expert, tile_rows.astype(jnp.int32)


def _combine_kernel(pos_ref, x_ref, w_ref, ys_hbm, gain_ref, o_ref, buf, sems, *, final_norm):
    i = pl.program_id(0)
    tc = x_ref.shape[0]

    def start(step, slot):
        def body(r, carry):
            t = step * tc + r
            for k in range(TOP_K):
                _row_copy(ys_hbm, buf.at[slot, k], pos_ref[TOP_K * t + k], r, sems.at[slot]).start()
            return carry
        lax.fori_loop(0, tc, body, 0, unroll=4)

    def wait(slot):
        def body(r, carry):
            for k in range(TOP_K):
                _row_copy(ys_hbm, buf.at[slot, k], 0, 0, sems.at[slot]).wait()
            return carry
        lax.fori_loop(0, tc, body, 0, unroll=4)

    slot = i % 2

    @pl.when(i == 0)
    def _():
        start(0, 0)

    @pl.when(i + 1 < pl.num_programs(0))
    def _():
        start(i + 1, 1 - slot)

    wait(slot)
    w = w_ref[...]
    out = x_ref[...] + (w[:, 0:1] * buf[slot, 0] + w[:, 1:2] * buf[slot, 1])
    if final_norm:
        out = _rms_scale(out) * gain_ref[...]
    o_ref[...] = out


def moe_combine(x, top_w, ys, dest, final_gain):
    m, d = x.shape
    tc = min(256, m)
    final_norm = final_gain is not None
    gain = (final_gain if final_norm else jnp.ones((d,), F32)).reshape(1, d)
    grid_spec = pltpu.PrefetchScalarGridSpec(
        num_scalar_prefetch=1,
        grid=(m // tc,),
        in_specs=[pl.BlockSpec((tc, d), lambda i, pos: (i, 0)),
                  pl.BlockSpec((tc, top_w.shape[1]), lambda i, pos: (i, 0)),
                  pl.BlockSpec(memory_space=pl.ANY),
                  pl.BlockSpec((1, d), lambda i, pos: (0, 0))],
        out_specs=pl.BlockSpec((tc, d), lambda i, pos: (i, 0)),
        scratch_shapes=[pltpu.VMEM((2, TOP_K, tc, d), F32), pltpu.SemaphoreType.DMA((2,))])
    return pl.pallas_call(
        functools.partial(_combine_kernel, final_norm=final_norm),
        grid_spec=grid_spec,
        out_shape=jax.ShapeDtypeStruct((m, d), F32),
        compiler_params=_params("arbitrary"),
        name="moe_combine",
    )(dest, x, top_w, ys, gain)


def moe_ffn(x, ln, router_w, router_b, wg, wu, wd, layer, final_gain):
    n_experts = router_w.shape[1]
    top_w, sel = router(x, ln, router_w, router_b)
    src_token, dest, tile_expert, tile_rows = dispatch_plan(sel, n_experts, EXPERT_ROW_TILE)
    ys = expert_ffn(x, src_token, tile_expert, tile_rows, ln, wg, wu, wd, layer)
    return moe_combine(x, top_w, ys, dest, final_gain)


def _final_norm_kernel(x_ref, g_ref, o_ref):
    o_ref[...] = _rms_scale(x_ref[...]) * g_ref[...]


def final_norm(x, gain):
    m, d = x.shape
    tm = min(ROW_TILE, m)
    return pl.pallas_call(
        _final_norm_kernel,
        grid=(m // tm,),
        in_specs=[pl.BlockSpec((tm, d), lambda i: (i, 0)), pl.BlockSpec((1, d), lambda i: (0, 0))],
        out_specs=pl.BlockSpec((tm, d), lambda i: (i, 0)),
        out_shape=jax.ShapeDtypeStruct((m, d), F32),
        compiler_params=_params("parallel"),
        name="final_norm",
    )(x, gain.reshape(1, d))


def kernel(x, ln_mix, ln_ffn, w_in, w_out, s5_lam_re, s5_lam_im, s5_log_dt, s5_b_re, s5_b_im,
           s5_c_re, s5_c_im, s5_d, s5_w_glu, s5_gain, ret_gain, ffn_w_gate, ffn_w_up, ffn_w_down,
           router_w, router_b, moe_w_gate, moe_w_up, moe_w_down, final_gain):
    batch, seq, d_model = x.shape
    depth = ln_mix.shape[0]
    d_s5, d_ret = s5_gain.shape[1], ret_gain.shape[1]
    groups = s5_lam_re.shape[1]
    t_len = min(S5_CHUNK, seq)
    n_chunks = seq // t_len
    tokens = batch * seq

    cos_t, sin_t = rope_tables(seq)
    s5_params = s5_layout_params(s5_lam_re, s5_lam_im, s5_log_dt, s5_b_re, s5_b_im,
                                 s5_c_re, s5_c_im, s5_d, s5_w_glu, t_len)
    w_in_b, w_out_b = w_in.astype(BF16), w_out.astype(BF16)
    ffn_wg_b, ffn_wu_b, ffn_wd_b = (ffn_w_gate.astype(BF16), ffn_w_up.astype(BF16),
                                    ffn_w_down.astype(BF16))
    moe_wg, moe_wu, moe_wd = moe_w_gate.astype(F32), moe_w_up.astype(F32), moe_w_down.astype(F32)
    xf = x.reshape(tokens, d_model).astype(F32)
    for i in range(depth):
        proj = norm_matmul(xf, ln_mix[i], w_in_b, i, BF16)
        u_t = s5_relayout_in(proj, d_s5, t_len)
        z = s5_groups(u_t, s5_params, i, batch)
        y_s5 = s5_relayout_out(z, t_len)
        y_ret = retention(proj, cos_t, sin_t, ret_gain[i], batch, seq, d_ret, d_s5 // d_ret)
        xf = out_proj(y_s5, y_ret, s5_gain[i], w_out_b, i, xf)
        j = i // 2
        last = i == depth - 1
        if i % 2 == 0:
            xf = dense_ffn(xf, ln_ffn[i], ffn_wg_b, ffn_wu_b, ffn_wd_b, j)
            if last:
                xf = final_norm(xf, final_gain)
        else:
            xf = moe_ffn(xf, ln_ffn[i], router_w[j], router_b[j], moe_wg, moe_wu, moe_wd, j,
                         final_gain if last else None)
    return xf.reshape(batch, seq, d_model).astype(x.dtype)
```

```python
import functools
import math

import jax
import jax.numpy as jnp
from jax import lax
from jax.experimental import pallas as pl
from jax.experimental.pallas import tpu as pltpu

F32 = jnp.float32
BF16 = jnp.bfloat16
HIGHEST = lax.Precision.HIGHEST

S5_GROUP = 16
S5_STATE = 64
S5_LAM_RE_MAX = -1e-4
RET_HEAD_DIM = 128
ROPE_BASE = 10000.0
TOP_K = 2
NORM_EPS = 1e-6
GN_EPS = 1e-5

VMEM_LIMIT_BYTES = 56 * 1024 * 1024

S5_CHUNK = 64
S5_RELAYOUT_ROWS = 16
S5_GROUPS_PER_STEP = 1
RET_CHUNK = 256
ROW_TILE = 1024
COL_TILE = 1024
FFN_ROW_TILE = 512
FFN_HID_TILE = 512
EXPERT_ROW_TILE = 1024
EXPERT_SUB_ROWS = 512
EXPERT_HID_TILE = 512


def _params(*semantics):
    return pltpu.CompilerParams(dimension_semantics=semantics,
                                vmem_limit_bytes=VMEM_LIMIT_BYTES)


def _rms_scale(x):
    return x * lax.rsqrt(jnp.mean(x * x, axis=-1, keepdims=True) + NORM_EPS)


def _norm_matmul_kernel(x_ref, g_ref, w_ref, o_ref, h_ref):
    @pl.when(pl.program_id(1) == 0)
    def _():
        h_ref[...] = (_rms_scale(x_ref[...]) * g_ref[...]).astype(BF16)

    o_ref[...] = jnp.dot(h_ref[...], w_ref[...],
                         preferred_element_type=F32).astype(o_ref.dtype)


def norm_matmul(x, gain, w, layer, out_dtype):
    m, k = x.shape
    n = w.shape[2]
    tm, tn = min(ROW_TILE, m), min(COL_TILE, n)
    return pl.pallas_call(
        _norm_matmul_kernel,
        grid=(m // tm, n // tn),
        in_specs=[pl.BlockSpec((tm, k), lambda i, j: (i, 0)),
                  pl.BlockSpec((1, k), lambda i, j: (0, 0)),
                  pl.BlockSpec((None, k, tn), lambda i, j: (layer, 0, j))],
        out_specs=pl.BlockSpec((tm, tn), lambda i, j: (i, j)),
        out_shape=jax.ShapeDtypeStruct((m, n), out_dtype),
        scratch_shapes=[pltpu.VMEM((tm, k), BF16)],
        compiler_params=_params("parallel", "arbitrary"),
        name="norm_matmul",
    )(x, gain.reshape(1, k), w)


def _bf16_terms(a):
    hi = a.astype(BF16)
    rest = a - hi.astype(F32)
    mid = rest.astype(BF16)
    lo = (rest - mid.astype(F32)).astype(BF16)
    return hi, mid, lo


def _select_dot(a, sel):
    return sum(jnp.dot(t, sel, preferred_element_type=F32) for t in _bf16_terms(a))


def _dot_select(sel, a):
    return sum(jnp.dot(sel, t, preferred_element_type=F32) for t in _bf16_terms(a))


def _s5_kernel(*refs, batch):
    for gi in range(refs[0].shape[0]):
        _s5_group(*[r.at[gi] for r in refs], batch=batch)


def _s5_group(u_ref, lrc_ref, lic_ref, lrr_ref, lir_ref, ldt_ref, cx_ref, cy_ref,
              btr_ref, bti_ref, dsk_ref, wglu_ref, z_ref,
              mt_ref, s_ref, xprev_ref, *, batch):
    rows, th = u_ref.shape
    hh = S5_GROUP
    t_len = th // hh
    p2 = 2 * S5_STATE
    n_chunks = rows // batch
    shift = hh.bit_length() - 1

    dt = jnp.exp(ldt_ref[...])

    lrc = jnp.minimum(lrc_ref[...], S5_LAM_RE_MAX) * dt
    lic = lic_ref[...] * dt
    jv = lax.broadcasted_iota(jnp.int32, (p2, t_len), 1).astype(F32)

    ar0 = jnp.exp(lrc * jv) * jnp.cos(lic * jv)
    ai0 = jnp.exp(lrc * jv) * jnp.sin(lic * jv)

    lane_th = lax.broadcasted_iota(jnp.int32, (t_len, th), 1)
    rep = jnp.where(jnp.right_shift(lane_th, shift)
                    == lax.broadcasted_iota(jnp.int32, (t_len, th), 0), 1.0, 0.0).astype(BF16)
    til = jnp.where(jnp.bitwise_and(lax.broadcasted_iota(jnp.int32, (hh, th), 1), hh - 1)
                    == lax.broadcasted_iota(jnp.int32, (hh, th), 0), 1.0, 0.0).astype(BF16)

    row_p = lax.broadcasted_iota(jnp.int32, (p2, 1), 0)
    top = row_p < S5_STATE
    cx = _select_dot(cx_ref[...], til)
    cy = _select_dot(cy_ref[...], til)
    cy = jnp.where(top, -cy, cy)

    ar0_e = _select_dot(ar0, rep)
    ai0_e = _select_dot(ai0, rep)
    a1_re = jnp.exp(lrc) * jnp.cos(lic)
    a1_im = jnp.exp(lrc) * jnp.sin(lic)
    ar1_e = ar0_e * a1_re - ai0_e * a1_im
    ai1_e = ar0_e * a1_im + ai0_e * a1_re
    ca0 = ar0_e * cx + ai0_e * cy
    ca1 = ar1_e * cx + ai1_e * cy
    sout_t = jnp.where(top, ca1, -ca1).astype(BF16)

    lrr = jnp.minimum(lrr_ref[...], S5_LAM_RE_MAX)
    lir = lir_ref[...]
    mag = jnp.exp(lrr * dt)
    ab_re = mag * jnp.cos(lir * dt)
    ab_im = mag * jnp.sin(lir * dt)
    num_re = ab_re - 1.0
    den = lrr * lrr + lir * lir
    f_re = (num_re * lrr + ab_im * lir) / den
    f_im = (ab_im * lrr - num_re * lir) / den
    b_re = btr_ref[...]
    b_im = bti_ref[...]
    bb_re = f_re * b_re - f_im * b_im
    bb_im = f_re * b_im + f_im * b_re
    left = lax.broadcasted_iota(jnp.int32, (1, p2), 1) < S5_STATE

    kt = jnp.dot(jnp.where(left, bb_re, -bb_im), ca0,
                 precision=HIGHEST, preferred_element_type=F32)

    ktpad = jnp.concatenate([jnp.zeros_like(kt), kt], axis=1)
    per_vreg = 128 // hh
    rolled = [ktpad] + [pltpu.roll(ktpad, hh * r, axis=1) for r in range(1, per_vreg)]
    for s in range(t_len):
        q, r = divmod(s, per_vreg)
        mt_ref[pl.ds(s * hh, hh), :] = rolled[r][:, th - 128 * q: 2 * th - 128 * q].astype(BF16)

    sv = lax.broadcasted_iota(jnp.int32, (t_len, p2), 0).astype(F32)
    e_in = (t_len - 1.0) - sv
    mag_s = jnp.exp(lrr * dt * e_in)
    ang_s = lir * dt * e_in
    ars = mag_s * jnp.cos(ang_s)
    ais = mag_s * jnp.sin(ang_s)
    row_th = lax.broadcasted_iota(jnp.int32, (th, t_len), 0)
    rep_t = jnp.where(jnp.right_shift(row_th, shift)
                      == lax.broadcasted_iota(jnp.int32, (th, t_len), 1), 1.0, 0.0).astype(BF16)
    a_exp = _dot_select(rep_t, jnp.concatenate([ars, ais], axis=1))
    b_sel = jnp.concatenate([jnp.where(left, bb_re, bb_im), jnp.where(left, -bb_im, bb_re),
                             jnp.where(left, bb_im, bb_re), jnp.where(left, bb_re, -bb_im)],
                            axis=1)
    b_exp = jnp.tile(b_sel, (t_len, 1))
    ars_e, ais_e = a_exp[:, :p2], a_exp[:, p2:]
    sin_p = ars_e * b_exp[:, 0 * p2:1 * p2] + ais_e * b_exp[:, 1 * p2:2 * p2]
    sin_q = ars_e * b_exp[:, 2 * p2:3 * p2] + ais_e * b_exp[:, 3 * p2:4 * p2]
    sin_pq = jnp.concatenate([sin_p, sin_q], axis=1).astype(BF16)

    x = u_ref[...]
    y = jnp.dot(x, mt_ref[...], preferred_element_type=F32)
    inj = jnp.dot(x, sin_pq, preferred_element_type=F32)
    s_ref[0] = inj[:, 0:p2]
    s_ref[1] = inj[:, p2:2 * p2]

    at_mag = jnp.exp(lrr * dt * float(t_len))
    at_re = at_mag * jnp.cos(lir * dt * float(t_len))
    at_im = at_mag * jnp.sin(lir * dt * float(t_len))
    at_im_p = jnp.where(left, -at_im, at_im)
    st_p = jnp.zeros((batch, p2), F32)
    st_q = jnp.zeros((batch, p2), F32)
    for c in range(n_chunks):
        for b in range(batch):
            xprev_ref[pl.ds(b * n_chunks + c, 1), :] = st_p[b:b + 1]
        inj_p = s_ref[0, pl.ds(c, batch, stride=n_chunks), :]
        inj_q = s_ref[1, pl.ds(c, batch, stride=n_chunks), :]
        st_p, st_q = (at_re * st_p + at_im_p * st_q + inj_p,
                      at_re * st_q - at_im_p * st_p + inj_q)

    y = y + jnp.dot(xprev_ref[...].astype(BF16), sout_t, preferred_element_type=F32)
    y = y + dsk_ref[...] * x.astype(F32)

    yg = jax.nn.gelu(y, approximate=True)
    lanes = 128
    w_lane = _select_dot(wglu_ref[...], til[:, :lanes])
    w_blk = jnp.tile(w_lane, (per_vreg, 1))
    same_step = (jnp.right_shift(lax.broadcasted_iota(jnp.int32, (lanes, lanes), 0), shift)
                 == jnp.right_shift(lax.broadcasted_iota(jnp.int32, (lanes, lanes), 1), shift))
    w_blk = jnp.where(same_step, w_blk, 0.0).astype(BF16)
    yg_b = yg.astype(BF16)
    gate = jnp.concatenate(
        [jnp.dot(yg_b[:, q * lanes:(q + 1) * lanes], w_blk, preferred_element_type=F32)
         for q in range(th // lanes)], axis=1)
    z_ref[...] = (yg * jax.nn.sigmoid(gate)).astype(z_ref.dtype)


def _granule_transpose(w):
    n, lanes = w.shape
    assert lanes // S5_GROUP == 8 and n % 8 == 0
    w = w.reshape(n // 8, 8, lanes)
    row = lax.broadcasted_iota(jnp.int32, (1, 8, lanes), 1)
    gran = jnp.right_shift(lax.broadcasted_iota(jnp.int32, (1, 8, lanes), 2), S5_GROUP.bit_length() - 1)
    for d in (4, 2, 1):
        bit = d.bit_length() - 1
        row_bit = jnp.bitwise_and(jnp.right_shift(row, bit), 1)
        gran_bit = jnp.bitwise_and(jnp.right_shift(gran, bit), 1)
        from_above = pltpu.roll(pltpu.roll(w, d, axis=1), lanes - S5_GROUP * d, axis=2)
        from_below = pltpu.roll(pltpu.roll(w, 8 - d, axis=1), S5_GROUP * d, axis=2)
        w = jnp.where(row_bit == gran_bit, w, jnp.where(row_bit == 1, from_above, from_below))
    return w.reshape(n, lanes)


def _lead_sublane_transpose(a):
    n_lead, m, sub, lanes = a.shape
    assert n_lead == 8 and sub == 8
    lead = lax.broadcasted_iota(jnp.int32, (n_lead, 1, sub, lanes), 0)
    row = lax.broadcasted_iota(jnp.int32, (n_lead, 1, sub, lanes), 2)

    def roll_rows(x, shift):
        return pltpu.roll(x.reshape(n_lead * m, sub, lanes), shift, axis=1).reshape(x.shape)

    for d in (4, 2, 1):
        bit = d.bit_length() - 1
        lead_bit = jnp.bitwise_and(jnp.right_shift(lead, bit), 1)
        row_bit = jnp.bitwise_and(jnp.right_shift(row, bit), 1)
        from_next = roll_rows(jnp.roll(a, -d, axis=0), d)
        from_prev = roll_rows(jnp.roll(a, d, axis=0), sub - d)
        a = jnp.where(lead_bit == row_bit, a, jnp.where(row_bit == 1, from_next, from_prev))
    return a


def _s5_relayout_in_kernel(u_ref, o_ref, *, t_len):
    n_groups, n_rows, _ = o_ref.shape
    n_hi = t_len // 8
    piece = S5_RELAYOUT_ROWS * t_len
    half = 8 * t_len

    def body(p, carry):
        tok0 = pl.multiple_of(p * piece, piece)
        w = _granule_transpose(u_ref[pl.ds(tok0, piece), :].astype(F32))
        halves = [_lead_sublane_transpose(w[k * half:(k + 1) * half].reshape(8, n_hi, n_groups, 128))
                  for k in range(S5_RELAYOUT_ROWS // 8)]
        r0 = pl.multiple_of(p * S5_RELAYOUT_ROWS, S5_RELAYOUT_ROWS)
        for g in range(n_groups):
            for th in range(n_hi):
                tile = jnp.concatenate([hv[g, th] for hv in halves], axis=0)
                o_ref[g, pl.ds(r0, S5_RELAYOUT_ROWS), th * 128:(th + 1) * 128] = tile.astype(o_ref.dtype)
        return carry

    lax.fori_loop(0, n_rows // S5_RELAYOUT_ROWS, body, 0)


def _s5_relayout_out_kernel(z_ref, o_ref, *, t_len):
    n_groups, n_rows, _ = z_ref.shape
    n_hi = t_len // 8
    piece = S5_RELAYOUT_ROWS * t_len

    def body(p, carry):
        r0 = pl.multiple_of(p * S5_RELAYOUT_ROWS, S5_RELAYOUT_ROWS)
        zt = [z_ref[g, pl.ds(r0, S5_RELAYOUT_ROWS), :].astype(F32) for g in range(n_groups)]
        halves = []
        for k in range(S5_RELAYOUT_ROWS // 8):
            a = jnp.stack([jnp.stack([zt[g][k * 8:(k + 1) * 8, th * 128:(th + 1) * 128]
                                      for th in range(n_hi)], axis=0)
                           for g in range(n_groups)], axis=0)
            halves.append(_lead_sublane_transpose(a).reshape(8 * t_len, 128))
        tok0 = pl.multiple_of(p * piece, piece)
        o_ref[pl.ds(tok0, piece), :] = _granule_transpose(
            jnp.concatenate(halves, axis=0)).astype(o_ref.dtype)
        return carry

    lax.fori_loop(0, n_rows // S5_RELAYOUT_ROWS, body, 0)


def s5_relayout_in(proj, d_s5, t_len):
    tokens = proj.shape[0]
    lane_groups = 128 // S5_GROUP
    rows = tokens // t_len
    return pl.pallas_call(
        functools.partial(_s5_relayout_in_kernel, t_len=t_len),
        grid=(d_s5 // 128,),
        in_specs=[pl.BlockSpec((tokens, 128), lambda i: (0, i))],
        out_specs=pl.BlockSpec((lane_groups, rows, t_len * S5_GROUP), lambda i: (i, 0, 0)),
        out_shape=jax.ShapeDtypeStruct((d_s5 // S5_GROUP, rows, t_len * S5_GROUP), BF16),
        compiler_params=_params("parallel"),
        name="s5_relayout_in",
    )(proj)


def s5_relayout_out(z, t_len):
    groups, rows, th = z.shape
    lane_groups = 128 // S5_GROUP
    tokens = rows * t_len
    return pl.pallas_call(
        functools.partial(_s5_relayout_out_kernel, t_len=t_len),
        grid=(groups // lane_groups,),
        in_specs=[pl.BlockSpec((lane_groups, rows, th), lambda i: (i, 0, 0))],
        out_specs=pl.BlockSpec((tokens, 128), lambda i: (0, i)),
        out_shape=jax.ShapeDtypeStruct((tokens, groups * S5_GROUP), BF16),
        compiler_params=_params("parallel"),
        name="s5_relayout_out",
    )(z)


def s5_layout_params(lam_re, lam_im, log_dt, b_re, b_im, c_re, c_im, d_skip, w_glu, t_len):
    dup = lambda a, axis: jnp.concatenate([a, a], axis=axis)
    lrc = dup(lam_re, 2)[..., None]
    lic = dup(lam_im, 2)[..., None]
    lrr = dup(lam_re, 2)[:, :, None, :]
    lir = dup(lam_im, 2)[:, :, None, :]
    ldt = log_dt[:, :, None, None]
    c_re_t = jnp.swapaxes(c_re, 2, 3)
    c_im_t = jnp.swapaxes(c_im, 2, 3)
    cx = jnp.concatenate([c_re_t, c_im_t], axis=2)
    cy = jnp.concatenate([c_im_t, c_re_t], axis=2)
    btr = dup(jnp.swapaxes(b_re, 2, 3), 3)
    bti = dup(jnp.swapaxes(b_im, 2, 3), 3)
    dsk = jnp.tile(d_skip, (1, 1, t_len))[:, :, None, :]
    return tuple(a.astype(F32) for a in (lrc, lic, lrr, lir, ldt, cx, cy, btr, bti, dsk, w_glu))


def s5_groups(u_t, params, layer, batch):
    groups, rows, th = u_t.shape
    p2 = 2 * S5_STATE
    gs = S5_GROUPS_PER_STEP

    def per_group(*tail):
        return pl.BlockSpec((gs,) + tail, lambda g: (g,) + (0,) * len(tail))

    def per_layer_group(a):
        tail = a.shape[2:]
        return pl.BlockSpec((None, gs) + tail, lambda g: (layer, g) + (0,) * len(tail))

    return pl.pallas_call(
        functools.partial(_s5_kernel, batch=batch),
        grid=(groups // gs,),
        in_specs=[per_group(rows, th)] + [per_layer_group(a) for a in params],
        out_specs=per_group(rows, th),
        out_shape=jax.ShapeDtypeStruct((groups, rows, th), BF16),
        scratch_shapes=[pltpu.VMEM((gs, th, th), BF16),
                        pltpu.VMEM((gs, 2, rows, p2), F32),
                        pltpu.VMEM((gs, rows, p2), F32)],
        compiler_params=_params("parallel"),
        name="s5_groups",
    )(u_t, *params)


def _rope_table_kernel(inv_ref, cos_ref, sin_ref):
    rows, width = cos_ref.shape
    pos = (pl.program_id(0) * rows
           + lax.broadcasted_iota(jnp.int32, (rows, width), 0)).astype(F32)
    ang = pos * inv_ref[...]
    first_half = lax.broadcasted_iota(jnp.int32, (rows, width), 1) < width // 2
    cos_ref[...] = jnp.cos(ang)
    sin_ref[...] = jnp.where(first_half, -jnp.sin(ang), jnp.sin(ang))


def rope_tables(seq):
    d = RET_HEAD_DIM
    inv_freq = ROPE_BASE ** (-jnp.arange(0, d, 2, dtype=F32) / d)
    inv2 = jnp.concatenate([inv_freq, inv_freq]).reshape(1, d)
    rows = min(seq, 512)
    return pl.pallas_call(
        _rope_table_kernel,
        grid=(seq // rows,),
        in_specs=[pl.BlockSpec((1, d), lambda i: (0, 0))],
        out_specs=[pl.BlockSpec((rows, d), lambda i: (i, 0))] * 2,
        out_shape=[jax.ShapeDtypeStruct((seq, d), F32)] * 2,
        compiler_params=_params("parallel"),
        name="rope_tables",
    )(inv2)


def _retention_kernel(q_ref, k_ref, v_ref, g_ref, cos_ref, sin_ref, gain_ref, o_ref,
                      state_ref, mask_ref):
    tc, width = q_ref.shape
    d = RET_HEAD_DIM
    heads = width // d
    log_gammas = [math.log(1.0 - 2.0 ** (-5.0 - h)) for h in range(heads)]

    @pl.when((pl.program_id(0) == 0) & (pl.program_id(1) == 0))
    def _():
        diff = (lax.broadcasted_iota(jnp.int32, (tc, tc), 0)
                - lax.broadcasted_iota(jnp.int32, (tc, tc), 1))
        causal = diff >= 0
        diff_f = jnp.where(causal, diff, 0).astype(F32)
        for h in range(heads):
            mask_ref[h] = jnp.where(causal, jnp.exp(diff_f * log_gammas[h]), 0.0)

    @pl.when(pl.program_id(1) == 0)
    def _():
        state_ref[...] = jnp.zeros_like(state_ref)

    cos = cos_ref[...]
    sin = sin_ref[...]
    n_col = lax.broadcasted_iota(jnp.int32, (tc, 1), 0).astype(F32)

    def rope(t):
        return t * cos + pltpu.roll(t, d // 2, axis=1) * sin

    for h in range(heads):
        log_g = log_gammas[h]
        sl = slice(h * d, (h + 1) * d)
        q = rope(q_ref[:, sl].astype(F32))
        k = rope(k_ref[:, sl].astype(F32) * (d ** -0.5))
        v = v_ref[:, sl].astype(BF16)
        scores = lax.dot_general(q.astype(BF16), k.astype(BF16), (((1,), (1,)), ((), ())),
                                 preferred_element_type=F32) * mask_ref[h]
        inner = jnp.dot(scores.astype(BF16), v, preferred_element_type=F32)
        state = state_ref[h]
        q_dec = q * jnp.exp((n_col + 1.0) * log_g)
        cross = jnp.dot(q_dec.astype(BF16), state.astype(BF16), preferred_element_type=F32)
        k_dec = k * jnp.exp((tc - 1.0 - n_col) * log_g)
        kv = lax.dot_general(k_dec.astype(BF16), v, (((0,), (0,)), ((), ())),
                             preferred_element_type=F32)
        state_ref[h] = math.exp(tc * log_g) * state + kv
        o = inner + cross
        mu = jnp.mean(o, axis=-1, keepdims=True)
        var = jnp.mean(jnp.square(o - mu), axis=-1, keepdims=True)
        o = (o - mu) * lax.rsqrt(var + GN_EPS) * gain_ref[:, sl]
        o_ref[:, sl] = (jax.nn.silu(g_ref[:, sl].astype(F32)) * o).astype(o_ref.dtype)


def retention(proj, cos_t, sin_t, gain, batch, seq, width, col0):
    tc = min(RET_CHUNK, seq)
    nc = seq // tc
    d = RET_HEAD_DIM

    def col_block(c):
        return pl.BlockSpec((tc, width), lambda b, i: (b * nc + i, c))

    return pl.pallas_call(
        _retention_kernel,
        grid=(batch, nc),
        in_specs=[col_block(col0), col_block(col0 + 1), col_block(col0 + 2), col_block(col0 + 3),
                  pl.BlockSpec((tc, d), lambda b, i: (i, 0)),
                  pl.BlockSpec((tc, d), lambda b, i: (i, 0)),
                  pl.BlockSpec((1, width), lambda b, i: (0, 0))],
        out_specs=pl.BlockSpec((tc, width), lambda b, i: (b * nc + i, 0)),
        out_shape=jax.ShapeDtypeStruct((batch * seq, width), BF16),
        scratch_shapes=[pltpu.VMEM((width // d, d, d), F32),
                        pltpu.VMEM((width // d, tc, tc), F32)],
        compiler_params=_params("arbitrary", "arbitrary"),
        name="retention",
    )(proj, proj, proj, proj, cos_t, sin_t, gain.reshape(1, width))


def _out_proj_kernel(s5_ref, ret_ref, gs_ref, w_ref, x_ref, o_ref, a_ref):
    @pl.when(pl.program_id(1) == 0)
    def _():
        half = s5_ref.shape[1]
        a_ref[:, :half] = (_rms_scale(s5_ref[...].astype(F32)) * gs_ref[...]).astype(BF16)
        a_ref[:, half:] = ret_ref[...]

    o_ref[...] = x_ref[...] + jnp.dot(a_ref[...], w_ref[...], preferred_element_type=F32)


def out_proj(y_s5, y_ret, s5_gain, w, layer, x):
    m, d_s5 = y_s5.shape
    d_ret = y_ret.shape[1]
    _, k, n = w.shape
    tm, tn = min(ROW_TILE, m), min(COL_TILE, n)
    return pl.pallas_call(
        _out_proj_kernel,
        grid=(m // tm, n // tn),
        in_specs=[pl.BlockSpec((tm, d_s5), lambda i, j: (i, 0)),
                  pl.BlockSpec((tm, d_ret), lambda i, j: (i, 0)),
                  pl.BlockSpec((1, d_s5), lambda i, j: (0, 0)),
                  pl.BlockSpec((None, k, tn), lambda i, j: (layer, 0, j)),
                  pl.BlockSpec((tm, tn), lambda i, j: (i, j))],
        out_specs=pl.BlockSpec((tm, tn), lambda i, j: (i, j)),
        out_shape=jax.ShapeDtypeStruct((m, n), F32),
        scratch_shapes=[pltpu.VMEM((tm, k), BF16)],
        compiler_params=_params("parallel", "arbitrary"),
        name="out_proj",
    )(y_s5, y_ret, s5_gain.reshape(1, d_s5), w, x)


def _swiglu_accumulate(x_ref, ln_ref, wg_ref, wu_ref, wd_ref, h_ref, acc_ref, active):
    @pl.when(pl.program_id(1) == 0)
    def _():
        h_ref[...] = (_rms_scale(x_ref[...]) * ln_ref[...]).astype(BF16)
        acc_ref[...] = jnp.zeros_like(acc_ref)

    @pl.when(active)
    def _():
        h = h_ref[...]
        gate = jnp.dot(h, wg_ref[...], preferred_element_type=F32)
        up = jnp.dot(h, wu_ref[...], preferred_element_type=F32)
        act = (jax.nn.silu(gate) * up).astype(BF16)
        acc_ref[...] += jnp.dot(act, wd_ref[...], preferred_element_type=F32)


def _dense_ffn_kernel(x_ref, ln_ref, wg_ref, wu_ref, wd_ref, o_ref, h_ref, acc_ref):
    _swiglu_accumulate(x_ref, ln_ref, wg_ref, wu_ref, wd_ref, h_ref, acc_ref, True)

    @pl.when(pl.program_id(1) == pl.num_programs(1) - 1)
    def _():
        o_ref[...] = x_ref[...] + acc_ref[...]


def dense_ffn(x, ln, wg, wu, wd, layer):
    m, d = x.shape
    f = wg.shape[2]
    tm, tf = min(FFN_ROW_TILE, m), min(FFN_HID_TILE, f)
    return pl.pallas_call(
        _dense_ffn_kernel,
        grid=(m // tm, f // tf),
        in_specs=[pl.BlockSpec((tm, d), lambda i, j: (i, 0)),
                  pl.BlockSpec((1, d), lambda i, j: (0, 0)),
                  pl.BlockSpec((None, d, tf), lambda i, j: (layer, 0, j)),
                  pl.BlockSpec((None, d, tf), lambda i, j: (layer, 0, j)),
                  pl.BlockSpec((None, tf, d), lambda i, j: (layer, j, 0))],
        out_specs=pl.BlockSpec((tm, d), lambda i, j: (i, 0)),
        out_shape=jax.ShapeDtypeStruct((m, d), F32),
        scratch_shapes=[pltpu.VMEM((tm, d), BF16), pltpu.VMEM((tm, d), F32)],
        compiler_params=_params("parallel", "arbitrary"),
        name="dense_ffn",
    )(x, ln.reshape(1, d), wg, wu, wd)


def _row_copy(src_hbm, dst_ref, src_row, dst_row, sem):
    return pltpu.make_async_copy(src_hbm.at[pl.ds(src_row, 1)], dst_ref.at[pl.ds(dst_row, 1)], sem)


def _expert_ffn_kernel(tile_expert_ref, tile_rows_ref, src_ref, x_hbm, ln_ref,
                       wg_ref, wu_ref, wd_ref, ys_hbm, xbuf, h_ref, acc_ref, gather_sem, out_sem):
    del tile_expert_ref
    i, j = pl.program_id(0), pl.program_id(1)
    n_tiles, n_hid = pl.num_programs(0), pl.num_programs(1)
    tm = xbuf.shape[0]
    rows_valid = tile_rows_ref[i]

    def gather_start(tile):
        def body(r, carry):
            _row_copy(x_hbm, xbuf, src_ref[tile * tm + r], r, gather_sem).start()
            return carry
        lax.fori_loop(0, tm, body, 0, unroll=8)

    def gather_wait():
        def body(r, carry):
            _row_copy(x_hbm, xbuf, 0, 0, gather_sem).wait()
            return carry
        lax.fori_loop(0, tm, body, 0, unroll=8)

    @pl.when(j == 0)
    def _():
        @pl.when((i == 0) & (rows_valid > 0))
        def _():
            gather_start(0)

        @pl.when(rows_valid > 0)
        def _():
            gather_wait()
            h_ref[...] = (_rms_scale(xbuf[...]) * ln_ref[...]).astype(BF16)

        acc_ref[...] = jnp.zeros_like(acc_ref)

    @pl.when((j == 1) & (i + 1 < n_tiles))
    def _():
        @pl.when(tile_rows_ref[i + 1] > 0)
        def _():
            gather_start(i + 1)

    @pl.when(rows_valid > 0)
    def _():
        wg = wg_ref[...].astype(BF16)
        wu = wu_ref[...].astype(BF16)
        wd = wd_ref[...].astype(BF16)
        for s in range(tm // EXPERT_SUB_ROWS):
            @pl.when(rows_valid > s * EXPERT_SUB_ROWS)
            def _():
                rows = pl.ds(s * EXPERT_SUB_ROWS, EXPERT_SUB_ROWS)
                h = h_ref[rows, :]
                gate = jnp.dot(h, wg, preferred_element_type=F32)
                up = jnp.dot(h, wu, preferred_element_type=F32)
                act = (jax.nn.silu(gate) * up).astype(BF16)
                acc_ref[rows, :] += jnp.dot(act, wd, preferred_element_type=F32)

    @pl.when(j == n_hid - 1)
    def _():
        out_copy = pltpu.make_async_copy(
            acc_ref, ys_hbm.at[pl.ds(pl.multiple_of(i * tm, tm), tm)], out_sem)
        out_copy.start()
        out_copy.wait()


def expert_ffn(x, src_token, tile_expert, tile_rows, ln, wg, wu, wd, layer):
    d = x.shape[1]
    r = src_token.shape[0]
    f = wg.shape[3]
    tm, tf = EXPERT_ROW_TILE, min(EXPERT_HID_TILE, f)
    nj = f // tf
    assert nj >= 2 and r % tm == 0 and tm % EXPERT_SUB_ROWS == 0

    def hid(j, tr, i):
        return jnp.where(tr[i] > 0, j, nj - 1)

    grid_spec = pltpu.PrefetchScalarGridSpec(
        num_scalar_prefetch=3,
        grid=(r // tm, nj),
        in_specs=[pl.BlockSpec(memory_space=pl.ANY),
                  pl.BlockSpec((1, d), lambda i, j, te, tr, st: (0, 0)),
                  pl.BlockSpec((None, None, d, tf),
                               lambda i, j, te, tr, st: (layer, te[i], 0, hid(j, tr, i))),
                  pl.BlockSpec((None, None, d, tf),
                               lambda i, j, te, tr, st: (layer, te[i], 0, hid(j, tr, i))),
                  pl.BlockSpec((None, None, tf, d),
                               lambda i, j, te, tr, st: (layer, te[i], hid(j, tr, i), 0))],
        out_specs=pl.BlockSpec(memory_space=pl.ANY),
        scratch_shapes=[pltpu.VMEM((tm, d), F32), pltpu.VMEM((tm, d), BF16),
                        pltpu.VMEM((tm, d), F32),
                        pltpu.SemaphoreType.DMA(()), pltpu.SemaphoreType.DMA(())])
    return pl.pallas_call(
        _expert_ffn_kernel,
        grid_spec=grid_spec,
        out_shape=jax.ShapeDtypeStruct((r, d), F32),
        compiler_params=_params("arbitrary", "arbitrary"),
        name="expert_ffn",
    )(tile_expert, tile_rows, src_token, x, ln.reshape(1, d), wg, wu, wd)


def _router_kernel(x_ref, ln_ref, w_ref, b_ref, gates_ref, sel_ref, *, n_experts):
    h = _rms_scale(x_ref[...]) * ln_ref[...]
    w = w_ref[...]
    h_hi, w_hi = h.astype(BF16), w.astype(BF16)
    h_lo = (h - h_hi.astype(F32)).astype(BF16)
    w_lo = (w - w_hi.astype(F32)).astype(BF16)
    logits = (jnp.dot(h_hi, w_hi, preferred_element_type=F32)
              + (jnp.dot(h_hi, w_lo, preferred_element_type=F32)
                 + jnp.dot(h_lo, w_hi, preferred_element_type=F32))) + b_ref[...]
    lane = lax.broadcasted_iota(jnp.int32, logits.shape, 1)
    width = logits.shape[1]
    logits = jnp.where(lane < n_experts, logits, -jnp.inf)
    top1 = jnp.max(logits, axis=-1, keepdims=True)
    idx1 = jnp.min(jnp.where(logits == top1, lane, width), axis=-1, keepdims=True)
    rest = jnp.where(lane == idx1, -jnp.inf, logits)
    top2 = jnp.max(rest, axis=-1, keepdims=True)
    idx2 = jnp.min(jnp.where(rest == top2, lane, width), axis=-1, keepdims=True)
    e2 = jnp.exp(top2 - top1)
    denom = 1.0 + e2
    gates_ref[...] = jnp.where(lane == 0, 1.0 / denom, jnp.where(lane == 1, e2 / denom, 0.0))
    sel_ref[...] = jnp.where(lane == 0, idx1, jnp.where(lane == 1, idx2, 0))


def router(x, ln, w, b):
    m, d = x.shape
    n_experts = w.shape[1]
    lanes = 128
    w_pad = jnp.pad(w.astype(F32), ((0, 0), (0, lanes - n_experts)))
    b_pad = jnp.pad(b.astype(F32), (0, lanes - n_experts)).reshape(1, lanes)
    tm = min(ROW_TILE, m)
    return pl.pallas_call(
        functools.partial(_router_kernel, n_experts=n_experts),
        grid=(m // tm,),
        in_specs=[pl.BlockSpec((tm, d), lambda i: (i, 0)),
                  pl.BlockSpec((1, d), lambda i: (0, 0)),
                  pl.BlockSpec((d, lanes), lambda i: (0, 0)),
                  pl.BlockSpec((1, lanes), lambda i: (0, 0))],
        out_specs=[pl.BlockSpec((tm, lanes), lambda i: (i, 0))] * 2,
        out_shape=[jax.ShapeDtypeStruct((m, lanes), F32),
                   jax.ShapeDtypeStruct((m, lanes), jnp.int32)],
        compiler_params=_params("parallel"),
        name="router",
    )(x, ln.reshape(1, d), w_pad, b_pad)


def dispatch_plan(sel, n_experts, tm):
    m = sel.shape[0]
    flat_e = sel[:, :TOP_K].reshape(-1)
    onehot = (flat_e[:, None] == jnp.arange(n_experts, dtype=jnp.int32)[None, :]).astype(jnp.int32)
    csum = jnp.cumsum(onehot, axis=0)
    rank = jnp.take_along_axis(csum, flat_e[:, None], axis=1)[:, 0] - 1
    counts = csum[-1]
    padded = ((counts + tm - 1) // tm) * tm
    ends = jnp.cumsum(padded)
    starts = ends - padded
    dest = (starts[flat_e] + rank).astype(jnp.int32)
    r = TOP_K * m + n_experts * tm
    src_token = jnp.zeros((r,), jnp.int32).at[dest].set(jnp.arange(TOP_K * m, dtype=jnp.int32) // TOP_K)
    tile_start = jnp.arange(r // tm, dtype=jnp.int32) * tm
    probe = jnp.minimum(tile_start, ends[-1] - 1)
    tile_expert = jnp.minimum(jnp.sum((ends[None, :] <= probe[:, None]).astype(jnp.int32), axis=1),
                              n_experts - 1).astype(jnp.int32)
    real_end = (starts + counts)[tile_expert]
    tile_rows = jnp.where(tile_start < ends[-1], jnp.clip(real_end - tile_start, 0, tm), 0)
    return src_token, dest, tile_expert, tile_rows.astype(jnp.int32)


def _combine_kernel(pos_ref, x_ref, w_ref, ys_hbm, gain_ref, o_ref, buf, sems, *, final_norm):
    i = pl.program_id(0)
    tc = x_ref.shape[0]

    def start(step, slot):
        def body(r, carry):
            t = step * tc + r
            for k in range(TOP_K):
                _row_copy(ys_hbm, buf.at[slot, k], pos_ref[TOP_K * t + k], r, sems.at[slot]).start()
            return carry
        lax.fori_loop(0, tc, body, 0, unroll=4)

    def wait(slot):
        def body(r, carry):
            for k in range(TOP_K):
                _row_copy(ys_hbm, buf.at[slot, k], 0, 0, sems.at[slot]).wait()
            return carry
        lax.fori_loop(0, tc, body, 0, unroll=4)

    slot = i % 2

    @pl.when(i == 0)
    def _():
        start(0, 0)

    @pl.when(i + 1 < pl.num_programs(0))
    def _():
        start(i + 1, 1 - slot)

    wait(slot)
    w = w_ref[...]
    out = x_ref[...] + (w[:, 0:1] * buf[slot, 0] + w[:, 1:2] * buf[slot, 1])
    if final_norm:
        out = _rms_scale(out) * gain_ref[...]
    o_ref[...] = out


def moe_combine(x, top_w, ys, dest, final_gain):
    m, d = x.shape
    tc = min(256, m)
    final_norm = final_gain is not None
    gain = (final_gain if final_norm else jnp.ones((d,), F32)).reshape(1, d)
    grid_spec = pltpu.PrefetchScalarGridSpec(
        num_scalar_prefetch=1,
        grid=(m // tc,),
        in_specs=[pl.BlockSpec((tc, d), lambda i, pos: (i, 0)),
                  pl.BlockSpec((tc, top_w.shape[1]), lambda i, pos: (i, 0)),
                  pl.BlockSpec(memory_space=pl.ANY),
                  pl.BlockSpec((1, d), lambda i, pos: (0, 0))],
        out_specs=pl.BlockSpec((tc, d), lambda i, pos: (i, 0)),
        scratch_shapes=[pltpu.VMEM((2, TOP_K, tc, d), F32), pltpu.SemaphoreType.DMA((2,))])
    return pl.pallas_call(
        functools.partial(_combine_kernel, final_norm=final_norm),
        grid_spec=grid_spec,
        out_shape=jax.ShapeDtypeStruct((m, d), F32),
        compiler_params=_params("arbitrary"),
        name="moe_combine",
    )(dest, x, top_w, ys, gain)


def moe_ffn(x, ln, router_w, router_b, wg, wu, wd, layer, final_gain):
    n_experts = router_w.shape[1]
    top_w, sel = router(x, ln, router_w, router_b)
    src_token, dest, tile_expert, tile_rows = dispatch_plan(sel, n_experts, EXPERT_ROW_TILE)
    ys = expert_ffn(x, src_token, tile_expert, tile_rows, ln, wg, wu, wd, layer)
    return moe_combine(x, top_w, ys, dest, final_gain)


def _final_norm_kernel(x_ref, g_ref, o_ref):
    o_ref[...] = _rms_scale(x_ref[...]) * g_ref[...]


def final_norm(x, gain):
    m, d = x.shape
    tm = min(ROW_TILE, m)
    return pl.pallas_call(
        _final_norm_kernel,
        grid=(m // tm,),
        in_specs=[pl.BlockSpec((tm, d), lambda i: (i, 0)), pl.BlockSpec((1, d), lambda i: (0, 0))],
        out_specs=pl.BlockSpec((tm, d), lambda i: (i, 0)),
        out_shape=jax.ShapeDtypeStruct((m, d), F32),
        compiler_params=_params("parallel"),
        name="final_norm",
    )(x, gain.reshape(1, d))


def kernel(x, ln_mix, ln_ffn, w_in, w_out, s5_lam_re, s5_lam_im, s5_log_dt, s5_b_re, s5_b_im,
           s5_c_re, s5_c_im, s5_d, s5_w_glu, s5_gain, ret_gain, ffn_w_gate, ffn_w_up, ffn_w_down,
           router_w, router_b, moe_w_gate, moe_w_up, moe_w_down, final_gain):
    batch, seq, d_model = x.shape
    depth = ln_mix.shape[0]
    d_s5, d_ret = s5_gain.shape[1], ret_gain.shape[1]
    groups = s5_lam_re.shape[1]
    t_len = min(S5_CHUNK, seq)
    n_chunks = seq // t_len
    tokens = batch * seq

    cos_t, sin_t = rope_tables(seq)
    s5_params = s5_layout_params(s5_lam_re, s5_lam_im, s5_log_dt, s5_b_re, s5_b_im,
                                 s5_c_re, s5_c_im, s5_d, s5_w_glu, t_len)
    w_in_b, w_out_b = w_in.astype(BF16), w_out.astype(BF16)
    ffn_wg_b, ffn_wu_b, ffn_wd_b = (ffn_w_gate.astype(BF16), ffn_w_up.astype(BF16),
                                    ffn_w_down.astype(BF16))
    moe_wg, moe_wu, moe_wd = moe_w_gate.astype(F32), moe_w_up.astype(F32), moe_w_down.astype(F32)
    xf = x.reshape(tokens, d_model).astype(F32)
    for i in range(depth):
        proj = norm_matmul(xf, ln_mix[i], w_in_b, i, BF16)
        u_t = s5_relayout_in(proj, d_s5, t_len)
        z = s5_groups(u_t, s5_params, i, batch)
        y_s5 = s5_relayout_out(z, t_len)
        y_ret = retention(proj, cos_t, sin_t, ret_gain[i], batch, seq, d_ret, d_s5 // d_ret)
        xf = out_proj(y_s5, y_ret, s5_gain[i], w_out_b, i, xf)
        j = i // 2
        last = i == depth - 1
        if i % 2 == 0:
            xf = dense_ffn(xf, ln_ffn[i], ffn_wg_b, ffn_wu_b, ffn_wd_b, j)
            if last:
                xf = final_norm(xf, final_gain)
        else:
            xf = moe_ffn(xf, ln_ffn[i], router_w[j], router_b[j], moe_wg, moe_wu, moe_wd, j,
                         final_gain if last else None)
    return xf.reshape(batch, seq, d_model).astype(x.dtype)
```

```python
import functools
import math

import jax
import jax.numpy as jnp
from jax import lax
from jax.experimental import pallas as pl
from jax.experimental.pallas import tpu as pltpu

F32 = jnp.float32
BF16 = jnp.bfloat16
HIGHEST = lax.Precision.HIGHEST

S5_GROUP = 16
S5_STATE = 64
S5_LAM_RE_MAX = -1e-4
RET_HEAD_DIM = 128
ROPE_BASE = 10000.0
TOP_K = 2
NORM_EPS = 1e-6
GN_EPS = 1e-5

VMEM_LIMIT_BYTES = 56 * 1024 * 1024

S5_CHUNK = 64
S5_RELAYOUT_ROWS = 16
S5_GROUPS_PER_STEP = 1
RET_CHUNK = 256
ROW_TILE = 1024
COL_TILE = 1024
FFN_ROW_TILE = 512
FFN_HID_TILE = 512
EXPERT_ROW_TILE = 1536
EXPERT_SUB_ROWS = 512
EXPERT_HID_TILE = 256


def _params(*semantics):
    return pltpu.CompilerParams(dimension_semantics=semantics,
                                vmem_limit_bytes=VMEM_LIMIT_BYTES)


def _rms_scale(x):
    return x * lax.rsqrt(jnp.mean(x * x, axis=-1, keepdims=True) + NORM_EPS)


def _norm_matmul_kernel(x_ref, g_ref, w_ref, o_ref, h_ref):
    @pl.when(pl.program_id(1) == 0)
    def _():
        h_ref[...] = (_rms_scale(x_ref[...]) * g_ref[...]).astype(BF16)

    o_ref[...] = jnp.dot(h_ref[...], w_ref[...],
                         preferred_element_type=F32).astype(o_ref.dtype)


def norm_matmul(x, gain, w, layer, out_dtype):
    m, k = x.shape
    n = w.shape[2]
    tm, tn = min(ROW_TILE, m), min(COL_TILE, n)
    return pl.pallas_call(
        _norm_matmul_kernel,
        grid=(m // tm, n // tn),
        in_specs=[pl.BlockSpec((tm, k), lambda i, j: (i, 0)),
                  pl.BlockSpec((1, k), lambda i, j: (0, 0)),
                  pl.BlockSpec((None, k, tn), lambda i, j: (layer, 0, j))],
        out_specs=pl.BlockSpec((tm, tn), lambda i, j: (i, j)),
        out_shape=jax.ShapeDtypeStruct((m, n), out_dtype),
        scratch_shapes=[pltpu.VMEM((tm, k), BF16)],
        compiler_params=_params("parallel", "arbitrary"),
        name="norm_matmul",
    )(x, gain.reshape(1, k), w)


def _bf16_terms(a):
    hi = a.astype(BF16)
    rest = a - hi.astype(F32)
    mid = rest.astype(BF16)
    lo = (rest - mid.astype(F32)).astype(BF16)
    return hi, mid, lo


def _select_dot(a, sel):
    return sum(jnp.dot(t, sel, preferred_element_type=F32) for t in _bf16_terms(a))


def _dot_select(sel, a):
    return sum(jnp.dot(sel, t, preferred_element_type=F32) for t in _bf16_terms(a))


def _s5_kernel(*refs, batch):
    for gi in range(refs[0].shape[0]):
        _s5_group(*[r.at[gi] for r in refs], batch=batch)


def _s5_group(u_ref, lrc_ref, lic_ref, lrr_ref, lir_ref, ldt_ref, cx_ref, cy_ref,
              btr_ref, bti_ref, dsk_ref, wglu_ref, z_ref,
              mt_ref, s_ref, xprev_ref, *, batch):
    rows, th = u_ref.shape
    hh = S5_GROUP
    t_len = th // hh
    p2 = 2 * S5_STATE
    n_chunks = rows // batch
    shift = hh.bit_length() - 1

    dt = jnp.exp(ldt_ref[...])

    lrc = jnp.minimum(lrc_ref[...], S5_LAM_RE_MAX) * dt
    lic = lic_ref[...] * dt
    jv = lax.broadcasted_iota(jnp.int32, (p2, t_len), 1).astype(F32)

    ar0 = jnp.exp(lrc * jv) * jnp.cos(lic * jv)
    ai0 = jnp.exp(lrc * jv) * jnp.sin(lic * jv)

    lane_th = lax.broadcasted_iota(jnp.int32, (t_len, th), 1)
    rep = jnp.where(jnp.right_shift(lane_th, shift)
                    == lax.broadcasted_iota(jnp.int32, (t_len, th), 0), 1.0, 0.0).astype(BF16)
    til = jnp.where(jnp.bitwise_and(lax.broadcasted_iota(jnp.int32, (hh, th), 1), hh - 1)
                    == lax.broadcasted_iota(jnp.int32, (hh, th), 0), 1.0, 0.0).astype(BF16)

    row_p = lax.broadcasted_iota(jnp.int32, (p2, 1), 0)
    top = row_p < S5_STATE
    cx = _select_dot(cx_ref[...], til)
    cy = _select_dot(cy_ref[...], til)
    cy = jnp.where(top, -cy, cy)

    ar0_e = _select_dot(ar0, rep)
    ai0_e = _select_dot(ai0, rep)
    a1_re = jnp.exp(lrc) * jnp.cos(lic)
    a1_im = jnp.exp(lrc) * jnp.sin(lic)
    ar1_e = ar0_e * a1_re - ai0_e * a1_im
    ai1_e = ar0_e * a1_im + ai0_e * a1_re
    ca0 = ar0_e * cx + ai0_e * cy
    ca1 = ar1_e * cx + ai1_e * cy
    sout_t = jnp.where(top, ca1, -ca1).astype(BF16)

    lrr = jnp.minimum(lrr_ref[...], S5_LAM_RE_MAX)
    lir = lir_ref[...]
    mag = jnp.exp(lrr * dt)
    ab_re = mag * jnp.cos(lir * dt)
    ab_im = mag * jnp.sin(lir * dt)
    num_re = ab_re - 1.0
    den = lrr * lrr + lir * lir
    f_re = (num_re * lrr + ab_im * lir) / den
    f_im = (ab_im * lrr - num_re * lir) / den
    b_re = btr_ref[...]
    b_im = bti_ref[...]
    bb_re = f_re * b_re - f_im * b_im
    bb_im = f_re * b_im + f_im * b_re
    left = lax.broadcasted_iota(jnp.int32, (1, p2), 1) < S5_STATE

    kt = jnp.dot(jnp.where(left, bb_re, -bb_im), ca0,
                 precision=HIGHEST, preferred_element_type=F32)

    ktpad = jnp.concatenate([jnp.zeros_like(kt), kt], axis=1)
    per_vreg = 128 // hh
    rolled = [ktpad] + [pltpu.roll(ktpad, hh * r, axis=1) for r in range(1, per_vreg)]
    for s in range(t_len):
        q, r = divmod(s, per_vreg)
        mt_ref[pl.ds(s * hh, hh), :] = rolled[r][:, th - 128 * q: 2 * th - 128 * q].astype(BF16)

    sv = lax.broadcasted_iota(jnp.int32, (t_len, p2), 0).astype(F32)
    e_in = (t_len - 1.0) - sv
    mag_s = jnp.exp(lrr * dt * e_in)
    ang_s = lir * dt * e_in
    ars = mag_s * jnp.cos(ang_s)
    ais = mag_s * jnp.sin(ang_s)
    row_th = lax.broadcasted_iota(jnp.int32, (th, t_len), 0)
    rep_t = jnp.where(jnp.right_shift(row_th, shift)
                      == lax.broadcasted_iota(jnp.int32, (th, t_len), 1), 1.0, 0.0).astype(BF16)
    a_exp = _dot_select(rep_t, jnp.concatenate([ars, ais], axis=1))
    b_sel = jnp.concatenate([jnp.where(left, bb_re, bb_im), jnp.where(left, -bb_im, bb_re),
                             jnp.where(left, bb_im, bb_re), jnp.where(left, bb_re, -bb_im)],
                            axis=1)
    b_exp = jnp.tile(b_sel, (t_len, 1))
    ars_e, ais_e = a_exp[:, :p2], a_exp[:, p2:]
    sin_p = ars_e * b_exp[:, 0 * p2:1 * p2] + ais_e * b_exp[:, 1 * p2:2 * p2]
    sin_q = ars_e * b_exp[:, 2 * p2:3 * p2] + ais_e * b_exp[:, 3 * p2:4 * p2]
    sin_pq = jnp.concatenate([sin_p, sin_q], axis=1).astype(BF16)

    x = u_ref[...]
    y = jnp.dot(x, mt_ref[...], preferred_element_type=F32)
    inj = jnp.dot(x, sin_pq, preferred_element_type=F32)
    s_ref[0] = inj[:, 0:p2]
    s_ref[1] = inj[:, p2:2 * p2]

    at_mag = jnp.exp(lrr * dt * float(t_len))
    at_re = at_mag * jnp.cos(lir * dt * float(t_len))
    at_im = at_mag * jnp.sin(lir * dt * float(t_len))
    at_im_p = jnp.where(left, -at_im, at_im)
    st_p = jnp.zeros((batch, p2), F32)
    st_q = jnp.zeros((batch, p2), F32)
    for c in range(n_chunks):
        for b in range(batch):
            xprev_ref[pl.ds(b * n_chunks + c, 1), :] = st_p[b:b + 1]
        inj_p = s_ref[0, pl.ds(c, batch, stride=n_chunks), :]
        inj_q = s_ref[1, pl.ds(c, batch, stride=n_chunks), :]
        st_p, st_q = (at_re * st_p + at_im_p * st_q + inj_p,
                      at_re * st_q - at_im_p * st_p + inj_q)

    y = y + jnp.dot(xprev_ref[...].astype(BF16), sout_t, preferred_element_type=F32)
    y = y + dsk_ref[...] * x.astype(F32)

    yg = jax.nn.gelu(y, approximate=True)
    lanes = 128
    w_lane = _select_dot(wglu_ref[...], til[:, :lanes])
    w_blk = jnp.tile(w_lane, (per_vreg, 1))
    same_step = (jnp.right_shift(lax.broadcasted_iota(jnp.int32, (lanes, lanes), 0), shift)
                 == jnp.right_shift(lax.broadcasted_iota(jnp.int32, (lanes, lanes), 1), shift))
    w_blk = jnp.where(same_step, w_blk, 0.0).astype(BF16)
    yg_b = yg.astype(BF16)
    gate = jnp.concatenate(
        [jnp.dot(yg_b[:, q * lanes:(q + 1) * lanes], w_blk, preferred_element_type=F32)
         for q in range(th // lanes)], axis=1)
    z_ref[...] = (yg * jax.nn.sigmoid(gate)).astype(z_ref.dtype)


def _granule_transpose(w):
    n, lanes = w.shape
    assert lanes // S5_GROUP == 8 and n % 8 == 0
    w = w.reshape(n // 8, 8, lanes)
    row = lax.broadcasted_iota(jnp.int32, (1, 8, lanes), 1)
    gran = jnp.right_shift(lax.broadcasted_iota(jnp.int32, (1, 8, lanes), 2), S5_GROUP.bit_length() - 1)
    for d in (4, 2, 1):
        bit = d.bit_length() - 1
        row_bit = jnp.bitwise_and(jnp.right_shift(row, bit), 1)
        gran_bit = jnp.bitwise_and(jnp.right_shift(gran, bit), 1)
        from_above = pltpu.roll(pltpu.roll(w, d, axis=1), lanes - S5_GROUP * d, axis=2)
        from_below = pltpu.roll(pltpu.roll(w, 8 - d, axis=1), S5_GROUP * d, axis=2)
        w = jnp.where(row_bit == gran_bit, w, jnp.where(row_bit == 1, from_above, from_below))
    return w.reshape(n, lanes)


def _lead_sublane_transpose(a):
    n_lead, m, sub, lanes = a.shape
    assert n_lead == 8 and sub == 8
    lead = lax.broadcasted_iota(jnp.int32, (n_lead, 1, sub, lanes), 0)
    row = lax.broadcasted_iota(jnp.int32, (n_lead, 1, sub, lanes), 2)

    def roll_rows(x, shift):
        return pltpu.roll(x.reshape(n_lead * m, sub, lanes), shift, axis=1).reshape(x.shape)

    for d in (4, 2, 1):
        bit = d.bit_length() - 1
        lead_bit = jnp.bitwise_and(jnp.right_shift(lead, bit), 1)
        row_bit = jnp.bitwise_and(jnp.right_shift(row, bit), 1)
        from_next = roll_rows(jnp.roll(a, -d, axis=0), d)
        from_prev = roll_rows(jnp.roll(a, d, axis=0), sub - d)
        a = jnp.where(lead_bit == row_bit, a, jnp.where(row_bit == 1, from_next, from_prev))
    return a


def _s5_relayout_in_kernel(u_ref, o_ref, *, t_len):
    n_groups, n_rows, _ = o_ref.shape
    n_hi = t_len // 8
    piece = S5_RELAYOUT_ROWS * t_len
    half = 8 * t_len

    def body(p, carry):
        tok0 = pl.multiple_of(p * piece, piece)
        w = _granule_transpose(u_ref[pl.ds(tok0, piece), :].astype(F32))
        halves = [_lead_sublane_transpose(w[k * half:(k + 1) * half].reshape(8, n_hi, n_groups, 128))
                  for k in range(S5_RELAYOUT_ROWS // 8)]
        r0 = pl.multiple_of(p * S5_RELAYOUT_ROWS, S5_RELAYOUT_ROWS)
        for g in range(n_groups):
            for th in range(n_hi):
                tile = jnp.concatenate([hv[g, th] for hv in halves], axis=0)
                o_ref[g, pl.ds(r0, S5_RELAYOUT_ROWS), th * 128:(th + 1) * 128] = tile.astype(o_ref.dtype)
        return carry

    lax.fori_loop(0, n_rows // S5_RELAYOUT_ROWS, body, 0)


def _s5_relayout_out_kernel(z_ref, o_ref, *, t_len):
    n_groups, n_rows, _ = z_ref.shape
    n_hi = t_len // 8
    piece = S5_RELAYOUT_ROWS * t_len

    def body(p, carry):
        r0 = pl.multiple_of(p * S5_RELAYOUT_ROWS, S5_RELAYOUT_ROWS)
        zt = [z_ref[g, pl.ds(r0, S5_RELAYOUT_ROWS), :].astype(F32) for g in range(n_groups)]
        halves = []
        for k in range(S5_RELAYOUT_ROWS // 8):
            a = jnp.stack([jnp.stack([zt[g][k * 8:(k + 1) * 8, th * 128:(th + 1) * 128]
                                      for th in range(n_hi)], axis=0)
                           for g in range(n_groups)], axis=0)
            halves.append(_lead_sublane_transpose(a).reshape(8 * t_len, 128))
        tok0 = pl.multiple_of(p * piece, piece)
        o_ref[pl.ds(tok0, piece), :] = _granule_transpose(
            jnp.concatenate(halves, axis=0)).astype(o_ref.dtype)
        return carry

    lax.fori_loop(0, n_rows // S5_RELAYOUT_ROWS, body, 0)


def s5_relayout_in(proj, d_s5, t_len):
    tokens = proj.shape[0]
    lane_groups = 128 // S5_GROUP
    rows = tokens // t_len
    return pl.pallas_call(
        functools.partial(_s5_relayout_in_kernel, t_len=t_len),
        grid=(d_s5 // 128,),
        in_specs=[pl.BlockSpec((tokens, 128), lambda i: (0, i))],
        out_specs=pl.BlockSpec((lane_groups, rows, t_len * S5_GROUP), lambda i: (i, 0, 0)),
        out_shape=jax.ShapeDtypeStruct((d_s5 // S5_GROUP, rows, t_len * S5_GROUP), BF16),
        compiler_params=_params("parallel"),
        name="s5_relayout_in",
    )(proj)


def s5_relayout_out(z, t_len):
    groups, rows, th = z.shape
    lane_groups = 128 // S5_GROUP
    tokens = rows * t_len
    return pl.pallas_call(
        functools.partial(_s5_relayout_out_kernel, t_len=t_len),
        grid=(groups // lane_groups,),
        in_specs=[pl.BlockSpec((lane_groups, rows, th), lambda i: (i, 0, 0))],
        out_specs=pl.BlockSpec((tokens, 128), lambda i: (0, i)),
        out_shape=jax.ShapeDtypeStruct((tokens, groups * S5_GROUP), BF16),
        compiler_params=_params("parallel"),
        name="s5_relayout_out",
    )(z)


def s5_layout_params(lam_re, lam_im, log_dt, b_re, b_im, c_re, c_im, d_skip, w_glu, t_len):
    dup = lambda a, axis: jnp.concatenate([a, a], axis=axis)
    lrc = dup(lam_re, 2)[..., None]
    lic = dup(lam_im, 2)[..., None]
    lrr = dup(lam_re, 2)[:, :, None, :]
    lir = dup(lam_im, 2)[:, :, None, :]
    ldt = log_dt[:, :, None, None]
    c_re_t = jnp.swapaxes(c_re, 2, 3)
    c_im_t = jnp.swapaxes(c_im, 2, 3)
    cx = jnp.concatenate([c_re_t, c_im_t], axis=2)
    cy = jnp.concatenate([c_im_t, c_re_t], axis=2)
    btr = dup(jnp.swapaxes(b_re, 2, 3), 3)
    bti = dup(jnp.swapaxes(b_im, 2, 3), 3)
    dsk = jnp.tile(d_skip, (1, 1, t_len))[:, :, None, :]
    return tuple(a.astype(F32) for a in (lrc, lic, lrr, lir, ldt, cx, cy, btr, bti, dsk, w_glu))


def s5_groups(u_t, params, layer, batch):
    groups, rows, th = u_t.shape
    p2 = 2 * S5_STATE
    gs = S5_GROUPS_PER_STEP

    def per_group(*tail):
        return pl.BlockSpec((gs,) + tail, lambda g: (g,) + (0,) * len(tail))

    def per_layer_group(a):
        tail = a.shape[2:]
        return pl.BlockSpec((None, gs) + tail, lambda g: (layer, g) + (0,) * len(tail))

    return pl.pallas_call(
        functools.partial(_s5_kernel, batch=batch),
        grid=(groups // gs,),
        in_specs=[per_group(rows, th)] + [per_layer_group(a) for a in params],
        out_specs=per_group(rows, th),
        out_shape=jax.ShapeDtypeStruct((groups, rows, th), BF16),
        scratch_shapes=[pltpu.VMEM((gs, th, th), BF16),
                        pltpu.VMEM((gs, 2, rows, p2), F32),
                        pltpu.VMEM((gs, rows, p2), F32)],
        compiler_params=_params("parallel"),
        name="s5_groups",
    )(u_t, *params)


def _rope_table_kernel(inv_ref, cos_ref, sin_ref):
    rows, width = cos_ref.shape
    pos = (pl.program_id(0) * rows
           + lax.broadcasted_iota(jnp.int32, (rows, width), 0)).astype(F32)
    ang = pos * inv_ref[...]
    first_half = lax.broadcasted_iota(jnp.int32, (rows, width), 1) < width // 2
    cos_ref[...] = jnp.cos(ang)
    sin_ref[...] = jnp.where(first_half, -jnp.sin(ang), jnp.sin(ang))


def rope_tables(seq):
    d = RET_HEAD_DIM
    inv_freq = ROPE_BASE ** (-jnp.arange(0, d, 2, dtype=F32) / d)
    inv2 = jnp.concatenate([inv_freq, inv_freq]).reshape(1, d)
    rows = min(seq, 512)
    return pl.pallas_call(
        _rope_table_kernel,
        grid=(seq // rows,),
        in_specs=[pl.BlockSpec((1, d), lambda i: (0, 0))],
        out_specs=[pl.BlockSpec((rows, d), lambda i: (i, 0))] * 2,
        out_shape=[jax.ShapeDtypeStruct((seq, d), F32)] * 2,
        compiler_params=_params("parallel"),
        name="rope_tables",
    )(inv2)


def _retention_kernel(q_ref, k_ref, v_ref, g_ref, cos_ref, sin_ref, gain_ref, o_ref,
                      state_ref, mask_ref):
    tc, width = q_ref.shape
    d = RET_HEAD_DIM
    heads = width // d
    log_gammas = [math.log(1.0 - 2.0 ** (-5.0 - h)) for h in range(heads)]

    @pl.when((pl.program_id(0) == 0) & (pl.program_id(1) == 0))
    def _():
        diff = (lax.broadcasted_iota(jnp.int32, (tc, tc), 0)
                - lax.broadcasted_iota(jnp.int32, (tc, tc), 1))
        causal = diff >= 0
        diff_f = jnp.where(causal, diff, 0).astype(F32)
        for h in range(heads):
            mask_ref[h] = jnp.where(causal, jnp.exp(diff_f * log_gammas[h]), 0.0)

    @pl.when(pl.program_id(1) == 0)
    def _():
        state_ref[...] = jnp.zeros_like(state_ref)

    cos = cos_ref[...]
    sin = sin_ref[...]
    n_col = lax.broadcasted_iota(jnp.int32, (tc, 1), 0).astype(F32)

    def rope(t):
        return t * cos + pltpu.roll(t, d // 2, axis=1) * sin

    for h in range(heads):
        log_g = log_gammas[h]
        sl = slice(h * d, (h + 1) * d)
        q = rope(q_ref[:, sl].astype(F32))
        k = rope(k_ref[:, sl].astype(F32) * (d ** -0.5))
        v = v_ref[:, sl].astype(BF16)
        scores = lax.dot_general(q.astype(BF16), k.astype(BF16), (((1,), (1,)), ((), ())),
                                 preferred_element_type=F32) * mask_ref[h]
        inner = jnp.dot(scores.astype(BF16), v, preferred_element_type=F32)
        state = state_ref[h]
        q_dec = q * jnp.exp((n_col + 1.0) * log_g)
        cross = jnp.dot(q_dec.astype(BF16), state.astype(BF16), preferred_element_type=F32)
        k_dec = k * jnp.exp((tc - 1.0 - n_col) * log_g)
        kv = lax.dot_general(k_dec.astype(BF16), v, (((0,), (0,)), ((), ())),
                             preferred_element_type=F32)
        state_ref[h] = math.exp(tc * log_g) * state + kv
        o = inner + cross
        mu = jnp.mean(o, axis=-1, keepdims=True)
        var = jnp.mean(jnp.square(o - mu), axis=-1, keepdims=True)
        o = (o - mu) * lax.rsqrt(var + GN_EPS) * gain_ref[:, sl]
        o_ref[:, sl] = (jax.nn.silu(g_ref[:, sl].astype(F32)) * o).astype(o_ref.dtype)


def retention(proj, cos_t, sin_t, gain, batch, seq, width, col0):
    tc = min(RET_CHUNK, seq)
    nc = seq // tc
    d = RET_HEAD_DIM

    def col_block(c):
        return pl.BlockSpec((tc, width), lambda b, i: (b * nc + i, c))

    return pl.pallas_call(
        _retention_kernel,
        grid=(batch, nc),
        in_specs=[col_block(col0), col_block(col0 + 1), col_block(col0 + 2), col_block(col0 + 3),
                  pl.BlockSpec((tc, d), lambda b, i: (i, 0)),
                  pl.BlockSpec((tc, d), lambda b, i: (i, 0)),
                  pl.BlockSpec((1, width), lambda b, i: (0, 0))],
        out_specs=pl.BlockSpec((tc, width), lambda b, i: (b * nc + i, 0)),
        out_shape=jax.ShapeDtypeStruct((batch * seq, width), BF16),
        scratch_shapes=[pltpu.VMEM((width // d, d, d), F32),
                        pltpu.VMEM((width // d, tc, tc), F32)],
        compiler_params=_params("arbitrary", "arbitrary"),
        name="retention",
    )(proj, proj, proj, proj, cos_t, sin_t, gain.reshape(1, width))


def _out_proj_kernel(s5_ref, ret_ref, gs_ref, w_ref, x_ref, o_ref, a_ref):
    @pl.when(pl.program_id(1) == 0)
    def _():
        half = s5_ref.shape[1]
        a_ref[:, :half] = (_rms_scale(s5_ref[...].astype(F32)) * gs_ref[...]).astype(BF16)
        a_ref[:, half:] = ret_ref[...]

    o_ref[...] = x_ref[...] + jnp.dot(a_ref[...], w_ref[...], preferred_element_type=F32)


def out_proj(y_s5, y_ret, s5_gain, w, layer, x):
    m, d_s5 = y_s5.shape
    d_ret = y_ret.shape[1]
    _, k, n = w.shape
    tm, tn = min(ROW_TILE, m), min(COL_TILE, n)
    return pl.pallas_call(
        _out_proj_kernel,
        grid=(m // tm, n // tn),
        in_specs=[pl.BlockSpec((tm, d_s5), lambda i, j: (i, 0)),
                  pl.BlockSpec((tm, d_ret), lambda i, j: (i, 0)),
                  pl.BlockSpec((1, d_s5), lambda i, j: (0, 0)),
                  pl.BlockSpec((None, k, tn), lambda i, j: (layer, 0, j)),
                  pl.BlockSpec((tm, tn), lambda i, j: (i, j))],
        out_specs=pl.BlockSpec((tm, tn), lambda i, j: (i, j)),
        out_shape=jax.ShapeDtypeStruct((m, n), F32),
        scratch_shapes=[pltpu.VMEM((tm, k), BF16)],
        compiler_params=_params("parallel", "arbitrary"),
        name="out_proj",
    )(y_s5, y_ret, s5_gain.reshape(1, d_s5), w, x)


def _swiglu_accumulate(x_ref, ln_ref, wg_ref, wu_ref, wd_ref, h_ref, acc_ref, active):
    @pl.when(pl.program_id(1) == 0)
    def _():
        h_ref[...] = (_rms_scale(x_ref[...]) * ln_ref[...]).astype(BF16)
        acc_ref[...] = jnp.zeros_like(acc_ref)

    @pl.when(active)
    def _():
        h = h_ref[...]
        gate = jnp.dot(h, wg_ref[...], preferred_element_type=F32)
        up = jnp.dot(h, wu_ref[...], preferred_element_type=F32)
        act = (jax.nn.silu(gate) * up).astype(BF16)
        acc_ref[...] += jnp.dot(act, wd_ref[...], preferred_element_type=F32)


def _dense_ffn_kernel(x_ref, ln_ref, wg_ref, wu_ref, wd_ref, o_ref, h_ref, acc_ref):
    _swiglu_accumulate(x_ref, ln_ref, wg_ref, wu_ref, wd_ref, h_ref, acc_ref, True)

    @pl.when(pl.program_id(1) == pl.num_programs(1) - 1)
    def _():
        o_ref[...] = x_ref[...] + acc_ref[...]


def dense_ffn(x, ln, wg, wu, wd, layer):
    m, d = x.shape
    f = wg.shape[2]
    tm, tf = min(FFN_ROW_TILE, m), min(FFN_HID_TILE, f)
    return pl.pallas_call(
        _dense_ffn_kernel,
        grid=(m // tm, f // tf),
        in_specs=[pl.BlockSpec((tm, d), lambda i, j: (i, 0)),
                  pl.BlockSpec((1, d), lambda i, j: (0, 0)),
                  pl.BlockSpec((None, d, tf), lambda i, j: (layer, 0, j)),
                  pl.BlockSpec((None, d, tf), lambda i, j: (layer, 0, j)),
                  pl.BlockSpec((None, tf, d), lambda i, j: (layer, j, 0))],
        out_specs=pl.BlockSpec((tm, d), lambda i, j: (i, 0)),
        out_shape=jax.ShapeDtypeStruct((m, d), F32),
        scratch_shapes=[pltpu.VMEM((tm, d), BF16), pltpu.VMEM((tm, d), F32)],
        compiler_params=_params("parallel", "arbitrary"),
        name="dense_ffn",
    )(x, ln.reshape(1, d), wg, wu, wd)


def _row_copy(src_hbm, dst_ref, src_row, dst_row, sem):
    return pltpu.make_async_copy(src_hbm.at[pl.ds(src_row, 1)], dst_ref.at[pl.ds(dst_row, 1)], sem)


def _expert_ffn_kernel(tile_expert_ref, tile_rows_ref, src_ref, x_hbm, ln_ref,
                       wg_ref, wu_ref, wd_ref, ys_hbm, xbuf, h_ref, acc_ref, gather_sem, out_sem):
    del tile_expert_ref
    i, j = pl.program_id(0), pl.program_id(1)
    n_tiles, n_hid = pl.num_programs(0), pl.num_programs(1)
    tm = xbuf.shape[0]
    rows_valid = tile_rows_ref[i]

    def gather_start(tile):
        def body(r, carry):
            _row_copy(x_hbm, xbuf, src_ref[tile * tm + r], r, gather_sem).start()
            return carry
        lax.fori_loop(0, tm, body, 0, unroll=8)

    def gather_wait():
        def body(r, carry):
            _row_copy(x_hbm, xbuf, 0, 0, gather_sem).wait()
            return carry
        lax.fori_loop(0, tm, body, 0, unroll=8)

    @pl.when(j == 0)
    def _():
        @pl.when((i == 0) & (rows_valid > 0))
        def _():
            gather_start(0)

        @pl.when(rows_valid > 0)
        def _():
            gather_wait()
            h_ref[...] = (_rms_scale(xbuf[...]) * ln_ref[...]).astype(BF16)

        acc_ref[...] = jnp.zeros_like(acc_ref)

    @pl.when((j == 1) & (i + 1 < n_tiles))
    def _():
        @pl.when(tile_rows_ref[i + 1] > 0)
        def _():
            gather_start(i + 1)

    @pl.when(rows_valid > 0)
    def _():
        wg = wg_ref[...].astype(BF16)
        wu = wu_ref[...].astype(BF16)
        wd = wd_ref[...].astype(BF16)
        for s in range(tm // EXPERT_SUB_ROWS):
            @pl.when(rows_valid > s * EXPERT_SUB_ROWS)
            def _():
                rows = pl.ds(s * EXPERT_SUB_ROWS, EXPERT_SUB_ROWS)
                h = h_ref[rows, :]
                gate = jnp.dot(h, wg, preferred_element_type=F32)
                up = jnp.dot(h, wu, preferred_element_type=F32)
                act = (jax.nn.silu(gate) * up).astype(BF16)
                acc_ref[rows, :] += jnp.dot(act, wd, preferred_element_type=F32)

    @pl.when(j == n_hid - 1)
    def _():
        out_copy = pltpu.make_async_copy(
            acc_ref, ys_hbm.at[pl.ds(pl.multiple_of(i * tm, tm), tm)], out_sem)
        out_copy.start()
        out_copy.wait()


def expert_ffn(x, src_token, tile_expert, tile_rows, ln, wg, wu, wd, layer):
    d = x.shape[1]
    r = src_token.shape[0]
    f = wg.shape[3]
    tm, tf = EXPERT_ROW_TILE, min(EXPERT_HID_TILE, f)
    nj = f // tf
    assert nj >= 2 and r % tm == 0 and tm % EXPERT_SUB_ROWS == 0

    def hid(j, tr, i):
        return jnp.where(tr[i] > 0, j, nj - 1)

    grid_spec = pltpu.PrefetchScalarGridSpec(
        num_scalar_prefetch=3,
        grid=(r // tm, nj),
        in_specs=[pl.BlockSpec(memory_space=pl.ANY),
                  pl.BlockSpec((1, d), lambda i, j, te, tr, st: (0, 0)),
                  pl.BlockSpec((None, None, d, tf),
                               lambda i, j, te, tr, st: (layer, te[i], 0, hid(j, tr, i))),
                  pl.BlockSpec((None, None, d, tf),
                               lambda i, j, te, tr, st: (layer, te[i], 0, hid(j, tr, i))),
                  pl.BlockSpec((None, None, tf, d),
                               lambda i, j, te, tr, st: (layer, te[i], hid(j, tr, i), 0))],
        out_specs=pl.BlockSpec(memory_space=pl.ANY),
        scratch_shapes=[pltpu.VMEM((tm, d), F32), pltpu.VMEM((tm, d), BF16),
                        pltpu.VMEM((tm, d), F32),
                        pltpu.SemaphoreType.DMA(()), pltpu.SemaphoreType.DMA(())])
    return pl.pallas_call(
        _expert_ffn_kernel,
        grid_spec=grid_spec,
        out_shape=jax.ShapeDtypeStruct((r, d), F32),
        compiler_params=_params("arbitrary", "arbitrary"),
        name="expert_ffn",
    )(tile_expert, tile_rows, src_token, x, ln.reshape(1, d), wg, wu, wd)


def _router_kernel(x_ref, ln_ref, w_ref, b_ref, gates_ref, sel_ref, *, n_experts):
    h = _rms_scale(x_ref[...]) * ln_ref[...]
    w = w_ref[...]
    h_hi, w_hi = h.astype(BF16), w.astype(BF16)
    h_lo = (h - h_hi.astype(F32)).astype(BF16)
    w_lo = (w - w_hi.astype(F32)).astype(BF16)
    logits = (jnp.dot(h_hi, w_hi, preferred_element_type=F32)
              + (jnp.dot(h_hi, w_lo, preferred_element_type=F32)
                 + jnp.dot(h_lo, w_hi, preferred_element_type=F32))) + b_ref[...]
    lane = lax.broadcasted_iota(jnp.int32, logits.shape, 1)
    width = logits.shape[1]
    logits = jnp.where(lane < n_experts, logits, -jnp.inf)
    top1 = jnp.max(logits, axis=-1, keepdims=True)
    idx1 = jnp.min(jnp.where(logits == top1, lane, width), axis=-1, keepdims=True)
    rest = jnp.where(lane == idx1, -jnp.inf, logits)
    top2 = jnp.max(rest, axis=-1, keepdims=True)
    idx2 = jnp.min(jnp.where(rest == top2, lane, width), axis=-1, keepdims=True)
    e2 = jnp.exp(top2 - top1)
    denom = 1.0 + e2
    gates_ref[...] = jnp.where(lane == 0, 1.0 / denom, jnp.where(lane == 1, e2 / denom, 0.0))
    sel_ref[...] = jnp.where(lane == 0, idx1, jnp.where(lane == 1, idx2, 0))


def router(x, ln, w, b):
    m, d = x.shape
    n_experts = w.shape[1]
    lanes = 128
    w_pad = jnp.pad(w.astype(F32), ((0, 0), (0, lanes - n_experts)))
    b_pad = jnp.pad(b.astype(F32), (0, lanes - n_experts)).reshape(1, lanes)
    tm = min(ROW_TILE, m)
    return pl.pallas_call(
        functools.partial(_router_kernel, n_experts=n_experts),
        grid=(m // tm,),
        in_specs=[pl.BlockSpec((tm, d), lambda i: (i, 0)),
                  pl.BlockSpec((1, d), lambda i: (0, 0)),
                  pl.BlockSpec((d, lanes), lambda i: (0, 0)),
                  pl.BlockSpec((1, lanes), lambda i: (0, 0))],
        out_specs=[pl.BlockSpec((tm, lanes), lambda i: (i, 0))] * 2,
        out_shape=[jax.ShapeDtypeStruct((m, lanes), F32),
                   jax.ShapeDtypeStruct((m, lanes), jnp.int32)],
        compiler_params=_params("parallel"),
        name="router",
    )(x, ln.reshape(1, d), w_pad, b_pad)


def dispatch_plan(sel, n_experts, tm):
    m = sel.shape[0]
    flat_e = sel[:, :TOP_K].reshape(-1)
    onehot = (flat_e[:, None] == jnp.arange(n_experts, dtype=jnp.int32)[None, :]).astype(jnp.int32)
    csum = jnp.cumsum(onehot, axis=0)
    rank = jnp.take_along_axis(csum, flat_e[:, None], axis=1)[:, 0] - 1
    counts = csum[-1]
    padded = ((counts + tm - 1) // tm) * tm
    ends = jnp.cumsum(padded)
    starts = ends - padded
    dest = (starts[flat_e] + rank).astype(jnp.int32)
    r = -(-(TOP_K * m + n_experts * (tm - 1)) // tm) * tm
    src_token = jnp.zeros((r,), jnp.int32).at[dest].set(jnp.arange(TOP_K * m, dtype=jnp.int32) // TOP_K)
    tile_start = jnp.arange(r // tm, dtype=jnp.int32) * tm
    probe = jnp.minimum(tile_start, ends[-1] - 1)
    tile_expert = jnp.minimum(jnp.sum((ends[None, :] <= probe[:, None]).astype(jnp.int32), axis=1),
                              n_experts - 1).astype(jnp.int32)
    real_end = (starts + counts)[tile_expert]
    tile_rows = jnp.where(tile_start < ends[-1], jnp.clip(real_end - tile_start, 0, tm), 0)
    return src_token, dest, tile_expert, tile_rows.astype(jnp.int32)


def _combine_kernel(pos_ref, x_ref, w_ref, ys_hbm, gain_ref, o_ref, buf, sems, *, final_norm):
    i = pl.program_id(0)
    tc = x_ref.shape[0]

    def start(step, slot):
        def body(r, carry):
            t = step * tc + r
            for k in range(TOP_K):
                _row_copy(ys_hbm, buf.at[slot, k], pos_ref[TOP_K * t + k], r, sems.at[slot]).start()
            return carry
        lax.fori_loop(0, tc, body, 0, unroll=4)

    def wait(slot):
        def body(r, carry):
            for k in range(TOP_K):
                _row_copy(ys_hbm, buf.at[slot, k], 0, 0, sems.at[slot]).wait()
            return carry
        lax.fori_loop(0, tc, body, 0, unroll=4)

    slot = i % 2

    @pl.when(i == 0)
    def _():
        start(0, 0)

    @pl.when(i + 1 < pl.num_programs(0))
    def _():
        start(i + 1, 1 - slot)

    wait(slot)
    w = w_ref[...]
    out = x_ref[...] + (w[:, 0:1] * buf[slot, 0] + w[:, 1:2] * buf[slot, 1])
    if final_norm:
        out = _rms_scale(out) * gain_ref[...]
    o_ref[...] = out


def moe_combine(x, top_w, ys, dest, final_gain):
    m, d = x.shape
    tc = min(256, m)
    final_norm = final_gain is not None
    gain = (final_gain if final_norm else jnp.ones((d,), F32)).reshape(1, d)
    grid_spec = pltpu.PrefetchScalarGridSpec(
        num_scalar_prefetch=1,
        grid=(m // tc,),
        in_specs=[pl.BlockSpec((tc, d), lambda i, pos: (i, 0)),
                  pl.BlockSpec((tc, top_w.shape[1]), lambda i, pos: (i, 0)),
                  pl.BlockSpec(memory_space=pl.ANY),
                  pl.BlockSpec((1, d), lambda i, pos: (0, 0))],
        out_specs=pl.BlockSpec((tc, d), lambda i, pos: (i, 0)),
        scratch_shapes=[pltpu.VMEM((2, TOP_K, tc, d), F32), pltpu.SemaphoreType.DMA((2,))])
    return pl.pallas_call(
        functools.partial(_combine_kernel, final_norm=final_norm),
        grid_spec=grid_spec,
        out_shape=jax.ShapeDtypeStruct((m, d), F32),
        compiler_params=_params("arbitrary"),
        name="moe_combine",
    )(dest, x, top_w, ys, gain)


def moe_ffn(x, ln, router_w, router_b, wg, wu, wd, layer, final_gain):
    n_experts = router_w.shape[1]
    top_w, sel = router(x, ln, router_w, router_b)
    src_token, dest, tile_expert, tile_rows = dispatch_plan(sel, n_experts, EXPERT_ROW_TILE)
    ys = expert_ffn(x, src_token, tile_expert, tile_rows, ln, wg, wu, wd, layer)
    return moe_combine(x, top_w, ys, dest, final_gain)


def _final_norm_kernel(x_ref, g_ref, o_ref):
    o_ref[...] = _rms_scale(x_ref[...]) * g_ref[...]


def final_norm(x, gain):
    m, d = x.shape
    tm = min(ROW_TILE, m)
    return pl.pallas_call(
        _final_norm_kernel,
        grid=(m // tm,),
        in_specs=[pl.BlockSpec((tm, d), lambda i: (i, 0)), pl.BlockSpec((1, d), lambda i: (0, 0))],
        out_specs=pl.BlockSpec((tm, d), lambda i: (i, 0)),
        out_shape=jax.ShapeDtypeStruct((m, d), F32),
        compiler_params=_params("parallel"),
        name="final_norm",
    )(x, gain.reshape(1, d))


def kernel(x, ln_mix, ln_ffn, w_in, w_out, s5_lam_re, s5_lam_im, s5_log_dt, s5_b_re, s5_b_im,
           s5_c_re, s5_c_im, s5_d, s5_w_glu, s5_gain, ret_gain, ffn_w_gate, ffn_w_up, ffn_w_down,
           router_w, router_b, moe_w_gate, moe_w_up, moe_w_down, final_gain):
    batch, seq, d_model = x.shape
    depth = ln_mix.shape[0]
    d_s5, d_ret = s5_gain.shape[1], ret_gain.shape[1]
    groups = s5_lam_re.shape[1]
    t_len = min(S5_CHUNK, seq)
    n_chunks = seq // t_len
    tokens = batch * seq

    cos_t, sin_t = rope_tables(seq)
    s5_params = s5_layout_params(s5_lam_re, s5_lam_im, s5_log_dt, s5_b_re, s5_b_im,
                                 s5_c_re, s5_c_im, s5_d, s5_w_glu, t_len)
    w_in_b, w_out_b = w_in.astype(BF16), w_out.astype(BF16)
    ffn_wg_b, ffn_wu_b, ffn_wd_b = (ffn_w_gate.astype(BF16), ffn_w_up.astype(BF16),
                                    ffn_w_down.astype(BF16))
    moe_wg, moe_wu, moe_wd = moe_w_gate.astype(F32), moe_w_up.astype(F32), moe_w_down.astype(F32)
    xf = x.reshape(tokens, d_model).astype(F32)
    for i in range(depth):
        proj = norm_matmul(xf, ln_mix[i], w_in_b, i, BF16)
        u_t = s5_relayout_in(proj, d_s5, t_len)
        z = s5_groups(u_t, s5_params, i, batch)
        y_s5 = s5_relayout_out(z, t_len)
        y_ret = retention(proj, cos_t, sin_t, ret_gain[i], batch, seq, d_ret, d_s5 // d_ret)
        xf = out_proj(y_s5, y_ret, s5_gain[i], w_out_b, i, xf)
        j = i // 2
        last = i == depth - 1
        if i % 2 == 0:
            xf = dense_ffn(xf, ln_ffn[i], ffn_wg_b, ffn_wu_b, ffn_wd_b, j)
            if last:
                xf = final_norm(xf, final_gain)
        else:
            xf = moe_ffn(xf, ln_ffn[i], router_w[j], router_b[j], moe_wg, moe_wu, moe_wd, j,
                         final_gain if last else None)
    return xf.reshape(batch, seq, d_model).astype(x.dtype)
```

```python
import functools
import math

import jax
import jax.numpy as jnp
from jax import lax
from jax.experimental import pallas as pl
from jax.experimental.pallas import tpu as pltpu

F32 = jnp.float32
BF16 = jnp.bfloat16
HIGHEST = lax.Precision.HIGHEST

S5_GROUP = 16
S5_STATE = 64
S5_LAM_RE_MAX = -1e-4
RET_HEAD_DIM = 128
ROPE_BASE = 10000.0
TOP_K = 2
NORM_EPS = 1e-6
GN_EPS = 1e-5

VMEM_LIMIT_BYTES = 56 * 1024 * 1024

S5_CHUNK = 64
S5_RELAYOUT_ROWS = 16
S5_GROUPS_PER_STEP = 2
COMBINE_ROW_TILE = 512
RET_CHUNK = 256
ROW_TILE = 1024
COL_TILE = 1024
FFN_ROW_TILE = 512
FFN_HID_TILE = 512
EXPERT_ROW_TILE = 1024
EXPERT_SUB_ROWS = 512
EXPERT_HID_TILE = 512


def _params(*semantics):
    return pltpu.CompilerParams(dimension_semantics=semantics,
                                vmem_limit_bytes=VMEM_LIMIT_BYTES)


def _rms_scale(x):
    return x * lax.rsqrt(jnp.mean(x * x, axis=-1, keepdims=True) + NORM_EPS)


def _norm_matmul_kernel(x_ref, g_ref, w_ref, o_ref, h_ref):
    @pl.when(pl.program_id(1) == 0)
    def _():
        h_ref[...] = (_rms_scale(x_ref[...]) * g_ref[...]).astype(BF16)

    o_ref[...] = jnp.dot(h_ref[...], w_ref[...],
                         preferred_element_type=F32).astype(o_ref.dtype)


def norm_matmul(x, gain, w, layer, out_dtype):
    m, k = x.shape
    n = w.shape[2]
    tm, tn = min(ROW_TILE, m), min(COL_TILE, n)
    return pl.pallas_call(
        _norm_matmul_kernel,
        grid=(m // tm, n // tn),
        in_specs=[pl.BlockSpec((tm, k), lambda i, j: (i, 0)),
                  pl.BlockSpec((1, k), lambda i, j: (0, 0)),
                  pl.BlockSpec((None, k, tn), lambda i, j: (layer, 0, j))],
        out_specs=pl.BlockSpec((tm, tn), lambda i, j: (i, j)),
        out_shape=jax.ShapeDtypeStruct((m, n), out_dtype),
        scratch_shapes=[pltpu.VMEM((tm, k), BF16)],
        compiler_params=_params("parallel", "arbitrary"),
        name="norm_matmul",
    )(x, gain.reshape(1, k), w)


def _bf16_terms(a):
    hi = a.astype(BF16)
    rest = a - hi.astype(F32)
    mid = rest.astype(BF16)
    lo = (rest - mid.astype(F32)).astype(BF16)
    return hi, mid, lo


def _select_dot(a, sel):
    return sum(jnp.dot(t, sel, preferred_element_type=F32) for t in _bf16_terms(a))


def _dot_select(sel, a):
    return sum(jnp.dot(sel, t, preferred_element_type=F32) for t in _bf16_terms(a))


def _s5_kernel(*refs, batch):
    for gi in range(refs[0].shape[0]):
        _s5_group(*[r.at[gi] for r in refs], batch=batch)


def _s5_group(u_ref, lrc_ref, lic_ref, lrr_ref, lir_ref, ldt_ref, cx_ref, cy_ref,
              btr_ref, bti_ref, dsk_ref, wglu_ref, z_ref,
              mt_ref, s_ref, xprev_ref, *, batch):
    rows, th = u_ref.shape
    hh = S5_GROUP
    t_len = th // hh
    p2 = 2 * S5_STATE
    n_chunks = rows // batch
    shift = hh.bit_length() - 1

    dt = jnp.exp(ldt_ref[...])

    lrc = jnp.minimum(lrc_ref[...], S5_LAM_RE_MAX) * dt
    lic = lic_ref[...] * dt
    jv = lax.broadcasted_iota(jnp.int32, (p2, t_len), 1).astype(F32)

    ar0 = jnp.exp(lrc * jv) * jnp.cos(lic * jv)
    ai0 = jnp.exp(lrc * jv) * jnp.sin(lic * jv)

    lane_th = lax.broadcasted_iota(jnp.int32, (t_len, th), 1)
    rep = jnp.where(jnp.right_shift(lane_th, shift)
                    == lax.broadcasted_iota(jnp.int32, (t_len, th), 0), 1.0, 0.0).astype(BF16)
    til = jnp.where(jnp.bitwise_and(lax.broadcasted_iota(jnp.int32, (hh, th), 1), hh - 1)
                    == lax.broadcasted_iota(jnp.int32, (hh, th), 0), 1.0, 0.0).astype(BF16)

    row_p = lax.broadcasted_iota(jnp.int32, (p2, 1), 0)
    top = row_p < S5_STATE
    cx = _select_dot(cx_ref[...], til)
    cy = _select_dot(cy_ref[...], til)
    cy = jnp.where(top, -cy, cy)

    ar0_e = _select_dot(ar0, rep)
    ai0_e = _select_dot(ai0, rep)
    a1_re = jnp.exp(lrc) * jnp.cos(lic)
    a1_im = jnp.exp(lrc) * jnp.sin(lic)
    ar1_e = ar0_e * a1_re - ai0_e * a1_im
    ai1_e = ar0_e * a1_im + ai0_e * a1_re
    ca0 = ar0_e * cx + ai0_e * cy
    ca1 = ar1_e * cx + ai1_e * cy
    sout_t = jnp.where(top, ca1, -ca1).astype(BF16)

    lrr = jnp.minimum(lrr_ref[...], S5_LAM_RE_MAX)
    lir = lir_ref[...]
    mag = jnp.exp(lrr * dt)
    ab_re = mag * jnp.cos(lir * dt)
    ab_im = mag * jnp.sin(lir * dt)
    num_re = ab_re - 1.0
    den = lrr * lrr + lir * lir
    f_re = (num_re * lrr + ab_im * lir) / den
    f_im = (ab_im * lrr - num_re * lir) / den
    b_re = btr_ref[...]
    b_im = bti_ref[...]
    bb_re = f_re * b_re - f_im * b_im
    bb_im = f_re * b_im + f_im * b_re
    left = lax.broadcasted_iota(jnp.int32, (1, p2), 1) < S5_STATE

    kt = jnp.dot(jnp.where(left, bb_re, -bb_im), ca0,
                 precision=HIGHEST, preferred_element_type=F32)

    ktpad = jnp.concatenate([jnp.zeros_like(kt), kt], axis=1)
    per_vreg = 128 // hh
    rolled = [ktpad] + [pltpu.roll(ktpad, hh * r, axis=1) for r in range(1, per_vreg)]
    for s in range(t_len):
        q, r = divmod(s, per_vreg)
        mt_ref[pl.ds(s * hh, hh), :] = rolled[r][:, th - 128 * q: 2 * th - 128 * q].astype(BF16)

    sv = lax.broadcasted_iota(jnp.int32, (t_len, p2), 0).astype(F32)
    e_in = (t_len - 1.0) - sv
    mag_s = jnp.exp(lrr * dt * e_in)
    ang_s = lir * dt * e_in
    ars = mag_s * jnp.cos(ang_s)
    ais = mag_s * jnp.sin(ang_s)
    row_th = lax.broadcasted_iota(jnp.int32, (th, t_len), 0)
    rep_t = jnp.where(jnp.right_shift(row_th, shift)
                      == lax.broadcasted_iota(jnp.int32, (th, t_len), 1), 1.0, 0.0).astype(BF16)
    a_exp = _dot_select(rep_t, jnp.concatenate([ars, ais], axis=1))
    b_sel = jnp.concatenate([jnp.where(left, bb_re, bb_im), jnp.where(left, -bb_im, bb_re),
                             jnp.where(left, bb_im, bb_re), jnp.where(left, bb_re, -bb_im)],
                            axis=1)
    b_exp = jnp.tile(b_sel, (t_len, 1))
    ars_e, ais_e = a_exp[:, :p2], a_exp[:, p2:]
    sin_p = ars_e * b_exp[:, 0 * p2:1 * p2] + ais_e * b_exp[:, 1 * p2:2 * p2]
    sin_q = ars_e * b_exp[:, 2 * p2:3 * p2] + ais_e * b_exp[:, 3 * p2:4 * p2]
    sin_pq = jnp.concatenate([sin_p, sin_q], axis=1).astype(BF16)

    x = u_ref[...]
    y = jnp.dot(x, mt_ref[...], preferred_element_type=F32)
    inj = jnp.dot(x, sin_pq, preferred_element_type=F32)
    s_ref[0] = inj[:, 0:p2]
    s_ref[1] = inj[:, p2:2 * p2]

    at_mag = jnp.exp(lrr * dt * float(t_len))
    at_re = at_mag * jnp.cos(lir * dt * float(t_len))
    at_im = at_mag * jnp.sin(lir * dt * float(t_len))
    at_im_p = jnp.where(left, -at_im, at_im)
    st_p = jnp.zeros((batch, p2), F32)
    st_q = jnp.zeros((batch, p2), F32)
    for c in range(n_chunks):
        for b in range(batch):
            xprev_ref[pl.ds(b * n_chunks + c, 1), :] = st_p[b:b + 1]
        inj_p = s_ref[0, pl.ds(c, batch, stride=n_chunks), :]
        inj_q = s_ref[1, pl.ds(c, batch, stride=n_chunks), :]
        st_p, st_q = (at_re * st_p + at_im_p * st_q + inj_p,
                      at_re * st_q - at_im_p * st_p + inj_q)

    y = y + jnp.dot(xprev_ref[...].astype(BF16), sout_t, preferred_element_type=F32)
    y = y + dsk_ref[...] * x.astype(F32)

    yg = jax.nn.gelu(y, approximate=True)
    lanes = 128
    w_lane = _select_dot(wglu_ref[...], til[:, :lanes])
    w_blk = jnp.tile(w_lane, (per_vreg, 1))
    same_step = (jnp.right_shift(lax.broadcasted_iota(jnp.int32, (lanes, lanes), 0), shift)
                 == jnp.right_shift(lax.broadcasted_iota(jnp.int32, (lanes, lanes), 1), shift))
    w_blk = jnp.where(same_step, w_blk, 0.0).astype(BF16)
    yg_b = yg.astype(BF16)
    gate = jnp.concatenate(
        [jnp.dot(yg_b[:, q * lanes:(q + 1) * lanes], w_blk, preferred_element_type=F32)
         for q in range(th // lanes)], axis=1)
    z_ref[...] = (yg * jax.nn.sigmoid(gate)).astype(z_ref.dtype)


def _granule_transpose(w):
    n, lanes = w.shape
    assert lanes // S5_GROUP == 8 and n % 8 == 0
    w = w.reshape(n // 8, 8, lanes)
    row = lax.broadcasted_iota(jnp.int32, (1, 8, lanes), 1)
    gran = jnp.right_shift(lax.broadcasted_iota(jnp.int32, (1, 8, lanes), 2), S5_GROUP.bit_length() - 1)
    for d in (4, 2, 1):
        bit = d.bit_length() - 1
        row_bit = jnp.bitwise_and(jnp.right_shift(row, bit), 1)
        gran_bit = jnp.bitwise_and(jnp.right_shift(gran, bit), 1)
        from_above = pltpu.roll(pltpu.roll(w, d, axis=1), lanes - S5_GROUP * d, axis=2)
        from_below = pltpu.roll(pltpu.roll(w, 8 - d, axis=1), S5_GROUP * d, axis=2)
        w = jnp.where(row_bit == gran_bit, w, jnp.where(row_bit == 1, from_above, from_below))
    return w.reshape(n, lanes)


def _lead_sublane_transpose(a):
    n_lead, m, sub, lanes = a.shape
    assert n_lead == 8 and sub == 8
    lead = lax.broadcasted_iota(jnp.int32, (n_lead, 1, sub, lanes), 0)
    row = lax.broadcasted_iota(jnp.int32, (n_lead, 1, sub, lanes), 2)

    def roll_rows(x, shift):
        return pltpu.roll(x.reshape(n_lead * m, sub, lanes), shift, axis=1).reshape(x.shape)

    for d in (4, 2, 1):
        bit = d.bit_length() - 1
        lead_bit = jnp.bitwise_and(jnp.right_shift(lead, bit), 1)
        row_bit = jnp.bitwise_and(jnp.right_shift(row, bit), 1)
        from_next = roll_rows(jnp.roll(a, -d, axis=0), d)
        from_prev = roll_rows(jnp.roll(a, d, axis=0), sub - d)
        a = jnp.where(lead_bit == row_bit, a, jnp.where(row_bit == 1, from_next, from_prev))
    return a


def _s5_relayout_in_kernel(u_ref, o_ref, *, t_len):
    n_groups, n_rows, _ = o_ref.shape
    n_hi = t_len // 8
    piece = S5_RELAYOUT_ROWS * t_len
    half = 8 * t_len

    def body(p, carry):
        tok0 = pl.multiple_of(p * piece, piece)
        w = _granule_transpose(u_ref[pl.ds(tok0, piece), :].astype(F32))
        halves = [_lead_sublane_transpose(w[k * half:(k + 1) * half].reshape(8, n_hi, n_groups, 128))
                  for k in range(S5_RELAYOUT_ROWS // 8)]
        r0 = pl.multiple_of(p * S5_RELAYOUT_ROWS, S5_RELAYOUT_ROWS)
        for g in range(n_groups):
            for th in range(n_hi):
                tile = jnp.concatenate([hv[g, th] for hv in halves], axis=0)
                o_ref[g, pl.ds(r0, S5_RELAYOUT_ROWS), th * 128:(th + 1) * 128] = tile.astype(o_ref.dtype)
        return carry

    lax.fori_loop(0, n_rows // S5_RELAYOUT_ROWS, body, 0)


def _s5_relayout_out_kernel(z_ref, o_ref, *, t_len):
    n_groups, n_rows, _ = z_ref.shape
    n_hi = t_len // 8
    piece = S5_RELAYOUT_ROWS * t_len

    def body(p, carry):
        r0 = pl.multiple_of(p * S5_RELAYOUT_ROWS, S5_RELAYOUT_ROWS)
        zt = [z_ref[g, pl.ds(r0, S5_RELAYOUT_ROWS), :].astype(F32) for g in range(n_groups)]
        halves = []
        for k in range(S5_RELAYOUT_ROWS // 8):
            a = jnp.stack([jnp.stack([zt[g][k * 8:(k + 1) * 8, th * 128:(th + 1) * 128]
                                      for th in range(n_hi)], axis=0)
                           for g in range(n_groups)], axis=0)
            halves.append(_lead_sublane_transpose(a).reshape(8 * t_len, 128))
        tok0 = pl.multiple_of(p * piece, piece)
        o_ref[pl.ds(tok0, piece), :] = _granule_transpose(
            jnp.concatenate(halves, axis=0)).astype(o_ref.dtype)
        return carry

    lax.fori_loop(0, n_rows // S5_RELAYOUT_ROWS, body, 0)


def s5_relayout_in(proj, d_s5, t_len):
    tokens = proj.shape[0]
    lane_groups = 128 // S5_GROUP
    rows = tokens // t_len
    return pl.pallas_call(
        functools.partial(_s5_relayout_in_kernel, t_len=t_len),
        grid=(d_s5 // 128,),
        in_specs=[pl.BlockSpec((tokens, 128), lambda i: (0, i))],
        out_specs=pl.BlockSpec((lane_groups, rows, t_len * S5_GROUP), lambda i: (i, 0, 0)),
        out_shape=jax.ShapeDtypeStruct((d_s5 // S5_GROUP, rows, t_len * S5_GROUP), BF16),
        compiler_params=_params("parallel"),
        name="s5_relayout_in",
    )(proj)


def s5_relayout_out(z, t_len):
    groups, rows, th = z.shape
    lane_groups = 128 // S5_GROUP
    tokens = rows * t_len
    return pl.pallas_call(
        functools.partial(_s5_relayout_out_kernel, t_len=t_len),
        grid=(groups // lane_groups,),
        in_specs=[pl.BlockSpec((lane_groups, rows, th), lambda i: (i, 0, 0))],
        out_specs=pl.BlockSpec((tokens, 128), lambda i: (0, i)),
        out_shape=jax.ShapeDtypeStruct((tokens, groups * S5_GROUP), BF16),
        compiler_params=_params("parallel"),
        name="s5_relayout_out",
    )(z)


def s5_layout_params(lam_re, lam_im, log_dt, b_re, b_im, c_re, c_im, d_skip, w_glu, t_len):
    dup = lambda a, axis: jnp.concatenate([a, a], axis=axis)
    lrc = dup(lam_re, 2)[..., None]
    lic = dup(lam_im, 2)[..., None]
    lrr = dup(lam_re, 2)[:, :, None, :]
    lir = dup(lam_im, 2)[:, :, None, :]
    ldt = log_dt[:, :, None, None]
    c_re_t = jnp.swapaxes(c_re, 2, 3)
    c_im_t = jnp.swapaxes(c_im, 2, 3)
    cx = jnp.concatenate([c_re_t, c_im_t], axis=2)
    cy = jnp.concatenate([c_im_t, c_re_t], axis=2)
    btr = dup(jnp.swapaxes(b_re, 2, 3), 3)
    bti = dup(jnp.swapaxes(b_im, 2, 3), 3)
    dsk = jnp.tile(d_skip, (1, 1, t_len))[:, :, None, :]
    return tuple(a.astype(F32) for a in (lrc, lic, lrr, lir, ldt, cx, cy, btr, bti, dsk, w_glu))


def s5_groups(u_t, params, layer, batch):
    groups, rows, th = u_t.shape
    p2 = 2 * S5_STATE
    gs = S5_GROUPS_PER_STEP

    def per_group(*tail):
        return pl.BlockSpec((gs,) + tail, lambda g: (g,) + (0,) * len(tail))

    def per_layer_group(a):
        tail = a.shape[2:]
        return pl.BlockSpec((None, gs) + tail, lambda g: (layer, g) + (0,) * len(tail))

    return pl.pallas_call(
        functools.partial(_s5_kernel, batch=batch),
        grid=(groups // gs,),
        in_specs=[per_group(rows, th)] + [per_layer_group(a) for a in params],
        out_specs=per_group(rows, th),
        out_shape=jax.ShapeDtypeStruct((groups, rows, th), BF16),
        scratch_shapes=[pltpu.VMEM((gs, th, th), BF16),
                        pltpu.VMEM((gs, 2, rows, p2), F32),
                        pltpu.VMEM((gs, rows, p2), F32)],
        compiler_params=_params("parallel"),
        name="s5_groups",
    )(u_t, *params)


def _rope_table_kernel(inv_ref, cos_ref, sin_ref):
    rows, width = cos_ref.shape
    pos = (pl.program_id(0) * rows
           + lax.broadcasted_iota(jnp.int32, (rows, width), 0)).astype(F32)
    ang = pos * inv_ref[...]
    first_half = lax.broadcasted_iota(jnp.int32, (rows, width), 1) < width // 2
    cos_ref[...] = jnp.cos(ang)
    sin_ref[...] = jnp.where(first_half, -jnp.sin(ang), jnp.sin(ang))


def rope_tables(seq):
    d = RET_HEAD_DIM
    inv_freq = ROPE_BASE ** (-jnp.arange(0, d, 2, dtype=F32) / d)
    inv2 = jnp.concatenate([inv_freq, inv_freq]).reshape(1, d)
    rows = min(seq, 512)
    return pl.pallas_call(
        _rope_table_kernel,
        grid=(seq // rows,),
        in_specs=[pl.BlockSpec((1, d), lambda i: (0, 0))],
        out_specs=[pl.BlockSpec((rows, d), lambda i: (i, 0))] * 2,
        out_shape=[jax.ShapeDtypeStruct((seq, d), F32)] * 2,
        compiler_params=_params("parallel"),
        name="rope_tables",
    )(inv2)


def _retention_kernel(q_ref, k_ref, v_ref, g_ref, cos_ref, sin_ref, gain_ref, o_ref,
                      state_ref, mask_ref):
    tc, width = q_ref.shape
    d = RET_HEAD_DIM
    heads = width // d
    log_gammas = [math.log(1.0 - 2.0 ** (-5.0 - h)) for h in range(heads)]

    @pl.when((pl.program_id(0) == 0) & (pl.program_id(1) == 0))
    def _():
        diff = (lax.broadcasted_iota(jnp.int32, (tc, tc), 0)
                - lax.broadcasted_iota(jnp.int32, (tc, tc), 1))
        causal = diff >= 0
        diff_f = jnp.where(causal, diff, 0).astype(F32)
        for h in range(heads):
            mask_ref[h] = jnp.where(causal, jnp.exp(diff_f * log_gammas[h]), 0.0)

    @pl.when(pl.program_id(1) == 0)
    def _():
        state_ref[...] = jnp.zeros_like(state_ref)

    cos = cos_ref[...]
    sin = sin_ref[...]
    n_col = lax.broadcasted_iota(jnp.int32, (tc, 1), 0).astype(F32)

    def rope(t):
        return t * cos + pltpu.roll(t, d // 2, axis=1) * sin

    for h in range(heads):
        log_g = log_gammas[h]
        sl = slice(h * d, (h + 1) * d)
        q = rope(q_ref[:, sl].astype(F32))
        k = rope(k_ref[:, sl].astype(F32) * (d ** -0.5))
        v = v_ref[:, sl].astype(BF16)
        scores = lax.dot_general(q.astype(BF16), k.astype(BF16), (((1,), (1,)), ((), ())),
                                 preferred_element_type=F32) * mask_ref[h]
        inner = jnp.dot(scores.astype(BF16), v, preferred_element_type=F32)
        state = state_ref[h]
        q_dec = q * jnp.exp((n_col + 1.0) * log_g)
        cross = jnp.dot(q_dec.astype(BF16), state.astype(BF16), preferred_element_type=F32)
        k_dec = k * jnp.exp((tc - 1.0 - n_col) * log_g)
        kv = lax.dot_general(k_dec.astype(BF16), v, (((0,), (0,)), ((), ())),
                             preferred_element_type=F32)
        state_ref[h] = math.exp(tc * log_g) * state + kv
        o = inner + cross
        mu = jnp.mean(o, axis=-1, keepdims=True)
        var = jnp.mean(jnp.square(o - mu), axis=-1, keepdims=True)
        o = (o - mu) * lax.rsqrt(var + GN_EPS) * gain_ref[:, sl]
        o_ref[:, sl] = (jax.nn.silu(g_ref[:, sl].astype(F32)) * o).astype(o_ref.dtype)


def retention(proj, cos_t, sin_t, gain, batch, seq, width, col0):
    tc = min(RET_CHUNK, seq)
    nc = seq // tc
    d = RET_HEAD_DIM

    def col_block(c):
        return pl.BlockSpec((tc, width), lambda b, i: (b * nc + i, c))

    return pl.pallas_call(
        _retention_kernel,
        grid=(batch, nc),
        in_specs=[col_block(col0), col_block(col0 + 1), col_block(col0 + 2), col_block(col0 + 3),
                  pl.BlockSpec((tc, d), lambda b, i: (i, 0)),
                  pl.BlockSpec((tc, d), lambda b, i: (i, 0)),
                  pl.BlockSpec((1, width), lambda b, i: (0, 0))],
        out_specs=pl.BlockSpec((tc, width), lambda b, i: (b * nc + i, 0)),
        out_shape=jax.ShapeDtypeStruct((batch * seq, width), BF16),
        scratch_shapes=[pltpu.VMEM((width // d, d, d), F32),
                        pltpu.VMEM((width // d, tc, tc), F32)],
        compiler_params=_params("arbitrary", "arbitrary"),
        name="retention",
    )(proj, proj, proj, proj, cos_t, sin_t, gain.reshape(1, width))


def _out_proj_kernel(s5_ref, ret_ref, gs_ref, w_ref, x_ref, o_ref, a_ref):
    @pl.when(pl.program_id(1) == 0)
    def _():
        half = s5_ref.shape[1]
        a_ref[:, :half] = (_rms_scale(s5_ref[...].astype(F32)) * gs_ref[...]).astype(BF16)
        a_ref[:, half:] = ret_ref[...]

    o_ref[...] = x_ref[...] + jnp.dot(a_ref[...], w_ref[...], preferred_element_type=F32)


def out_proj(y_s5, y_ret, s5_gain, w, layer, x):
    m, d_s5 = y_s5.shape
    d_ret = y_ret.shape[1]
    _, k, n = w.shape
    tm, tn = min(ROW_TILE, m), min(COL_TILE, n)
    return pl.pallas_call(
        _out_proj_kernel,
        grid=(m // tm, n // tn),
        in_specs=[pl.BlockSpec((tm, d_s5), lambda i, j: (i, 0)),
                  pl.BlockSpec((tm, d_ret), lambda i, j: (i, 0)),
                  pl.BlockSpec((1, d_s5), lambda i, j: (0, 0)),
                  pl.BlockSpec((None, k, tn), lambda i, j: (layer, 0, j)),
                  pl.BlockSpec((tm, tn), lambda i, j: (i, j))],
        out_specs=pl.BlockSpec((tm, tn), lambda i, j: (i, j)),
        out_shape=jax.ShapeDtypeStruct((m, n), F32),
        scratch_shapes=[pltpu.VMEM((tm, k), BF16)],
        compiler_params=_params("parallel", "arbitrary"),
        name="out_proj",
    )(y_s5, y_ret, s5_gain.reshape(1, d_s5), w, x)


def _swiglu_accumulate(x_ref, ln_ref, wg_ref, wu_ref, wd_ref, h_ref, acc_ref, active):
    @pl.when(pl.program_id(1) == 0)
    def _():
        h_ref[...] = (_rms_scale(x_ref[...]) * ln_ref[...]).astype(BF16)
        acc_ref[...] = jnp.zeros_like(acc_ref)

    @pl.when(active)
    def _():
        h = h_ref[...]
        gate = jnp.dot(h, wg_ref[...], preferred_element_type=F32)
        up = jnp.dot(h, wu_ref[...], preferred_element_type=F32)
        act = (jax.nn.silu(gate) * up).astype(BF16)
        acc_ref[...] += jnp.dot(act, wd_ref[...], preferred_element_type=F32)


def _dense_ffn_kernel(x_ref, ln_ref, wg_ref, wu_ref, wd_ref, o_ref, h_ref, acc_ref):
    _swiglu_accumulate(x_ref, ln_ref, wg_ref, wu_ref, wd_ref, h_ref, acc_ref, True)

    @pl.when(pl.program_id(1) == pl.num_programs(1) - 1)
    def _():
        o_ref[...] = x_ref[...] + acc_ref[...]


def dense_ffn(x, ln, wg, wu, wd, layer):
    m, d = x.shape
    f = wg.shape[2]
    tm, tf = min(FFN_ROW_TILE, m), min(FFN_HID_TILE, f)
    return pl.pallas_call(
        _dense_ffn_kernel,
        grid=(m // tm, f // tf),
        in_specs=[pl.BlockSpec((tm, d), lambda i, j: (i, 0)),
                  pl.BlockSpec((1, d), lambda i, j: (0, 0)),
                  pl.BlockSpec((None, d, tf), lambda i, j: (layer, 0, j)),
                  pl.BlockSpec((None, d, tf), lambda i, j: (layer, 0, j)),
                  pl.BlockSpec((None, tf, d), lambda i, j: (layer, j, 0))],
        out_specs=pl.BlockSpec((tm, d), lambda i, j: (i, 0)),
        out_shape=jax.ShapeDtypeStruct((m, d), F32),
        scratch_shapes=[pltpu.VMEM((tm, d), BF16), pltpu.VMEM((tm, d), F32)],
        compiler_params=_params("parallel", "arbitrary"),
        name="dense_ffn",
    )(x, ln.reshape(1, d), wg, wu, wd)


def _row_copy(src_hbm, dst_ref, src_row, dst_row, sem):
    return pltpu.make_async_copy(src_hbm.at[pl.ds(src_row, 1)], dst_ref.at[pl.ds(dst_row, 1)], sem)


def _expert_ffn_kernel(tile_expert_ref, tile_rows_ref, src_ref, x_hbm, ln_ref,
                       wg_ref, wu_ref, wd_ref, ys_hbm, xbuf, h_ref, acc_ref, gather_sem, out_sem):
    del tile_expert_ref
    i, j = pl.program_id(0), pl.program_id(1)
    n_tiles, n_hid = pl.num_programs(0), pl.num_programs(1)
    tm = xbuf.shape[0]
    rows_valid = tile_rows_ref[i]

    def gather_start(tile):
        def body(r, carry):
            _row_copy(x_hbm, xbuf, src_ref[tile * tm + r], r, gather_sem).start()
            return carry
        lax.fori_loop(0, tm, body, 0, unroll=8)

    def gather_wait():
        def body(r, carry):
            _row_copy(x_hbm, xbuf, 0, 0, gather_sem).wait()
            return carry
        lax.fori_loop(0, tm, body, 0, unroll=8)

    @pl.when(j == 0)
    def _():
        @pl.when((i == 0) & (rows_valid > 0))
        def _():
            gather_start(0)

        @pl.when(rows_valid > 0)
        def _():
            gather_wait()
            h_ref[...] = (_rms_scale(xbuf[...]) * ln_ref[...]).astype(BF16)

        acc_ref[...] = jnp.zeros_like(acc_ref)

    @pl.when((j == 1) & (i + 1 < n_tiles))
    def _():
        @pl.when(tile_rows_ref[i + 1] > 0)
        def _():
            gather_start(i + 1)

    @pl.when(rows_valid > 0)
    def _():
        wg = wg_ref[...].astype(BF16)
        wu = wu_ref[...].astype(BF16)
        wd = wd_ref[...].astype(BF16)
        for s in range(tm // EXPERT_SUB_ROWS):
            @pl.when(rows_valid > s * EXPERT_SUB_ROWS)
            def _():
                rows = pl.ds(s * EXPERT_SUB_ROWS, EXPERT_SUB_ROWS)
                h = h_ref[rows, :]
                gate = jnp.dot(h, wg, preferred_element_type=F32)
                up = jnp.dot(h, wu, preferred_element_type=F32)
                act = (jax.nn.silu(gate) * up).astype(BF16)
                acc_ref[rows, :] += jnp.dot(act, wd, preferred_element_type=F32)

    @pl.when(j == n_hid - 1)
    def _():
        out_copy = pltpu.make_async_copy(
            acc_ref, ys_hbm.at[pl.ds(pl.multiple_of(i * tm, tm), tm)], out_sem)
        out_copy.start()
        out_copy.wait()


def expert_ffn(x, src_token, tile_expert, tile_rows, ln, wg, wu, wd, layer):
    d = x.shape[1]
    r = src_token.shape[0]
    f = wg.shape[3]
    tm, tf = EXPERT_ROW_TILE, min(EXPERT_HID_TILE, f)
    nj = f // tf
    assert nj >= 2 and r % tm == 0 and tm % EXPERT_SUB_ROWS == 0

    def hid(j, tr, i):
        return jnp.where(tr[i] > 0, j, nj - 1)

    grid_spec = pltpu.PrefetchScalarGridSpec(
        num_scalar_prefetch=3,
        grid=(r // tm, nj),
        in_specs=[pl.BlockSpec(memory_space=pl.ANY),
                  pl.BlockSpec((1, d), lambda i, j, te, tr, st: (0, 0)),
                  pl.BlockSpec((None, None, d, tf),
                               lambda i, j, te, tr, st: (layer, te[i], 0, hid(j, tr, i))),
                  pl.BlockSpec((None, None, d, tf),
                               lambda i, j, te, tr, st: (layer, te[i], 0, hid(j, tr, i))),
                  pl.BlockSpec((None, None, tf, d),
                               lambda i, j, te, tr, st: (layer, te[i], hid(j, tr, i), 0))],
        out_specs=pl.BlockSpec(memory_space=pl.ANY),
        scratch_shapes=[pltpu.VMEM((tm, d), F32), pltpu.VMEM((tm, d), BF16),
                        pltpu.VMEM((tm, d), F32),
                        pltpu.SemaphoreType.DMA(()), pltpu.SemaphoreType.DMA(())])
    return pl.pallas_call(
        _expert_ffn_kernel,
        grid_spec=grid_spec,
        out_shape=jax.ShapeDtypeStruct((r, d), F32),
        compiler_params=_params("arbitrary", "arbitrary"),
        name="expert_ffn",
    )(tile_expert, tile_rows, src_token, x, ln.reshape(1, d), wg, wu, wd)


def _router_kernel(x_ref, ln_ref, w_ref, b_ref, gates_ref, sel_ref, *, n_experts):
    h = _rms_scale(x_ref[...]) * ln_ref[...]
    w = w_ref[...]
    h_hi, w_hi = h.astype(BF16), w.astype(BF16)
    h_lo = (h - h_hi.astype(F32)).astype(BF16)
    w_lo = (w - w_hi.astype(F32)).astype(BF16)
    logits = (jnp.dot(h_hi, w_hi, preferred_element_type=F32)
              + (jnp.dot(h_hi, w_lo, preferred_element_type=F32)
                 + jnp.dot(h_lo, w_hi, preferred_element_type=F32))) + b_ref[...]
    lane = lax.broadcasted_iota(jnp.int32, logits.shape, 1)
    width = logits.shape[1]
    logits = jnp.where(lane < n_experts, logits, -jnp.inf)
    top1 = jnp.max(logits, axis=-1, keepdims=True)
    idx1 = jnp.min(jnp.where(logits == top1, lane, width), axis=-1, keepdims=True)
    rest = jnp.where(lane == idx1, -jnp.inf, logits)
    top2 = jnp.max(rest, axis=-1, keepdims=True)
    idx2 = jnp.min(jnp.where(rest == top2, lane, width), axis=-1, keepdims=True)
    e2 = jnp.exp(top2 - top1)
    denom = 1.0 + e2
    gates_ref[...] = jnp.where(lane == 0, 1.0 / denom, jnp.where(lane == 1, e2 / denom, 0.0))
    sel_ref[...] = jnp.where(lane == 0, idx1, jnp.where(lane == 1, idx2, 0))


def router(x, ln, w, b):
    m, d = x.shape
    n_experts = w.shape[1]
    lanes = 128
    w_pad = jnp.pad(w.astype(F32), ((0, 0), (0, lanes - n_experts)))
    b_pad = jnp.pad(b.astype(F32), (0, lanes - n_experts)).reshape(1, lanes)
    tm = min(ROW_TILE, m)
    return pl.pallas_call(
        functools.partial(_router_kernel, n_experts=n_experts),
        grid=(m // tm,),
        in_specs=[pl.BlockSpec((tm, d), lambda i: (i, 0)),
                  pl.BlockSpec((1, d), lambda i: (0, 0)),
                  pl.BlockSpec((d, lanes), lambda i: (0, 0)),
                  pl.BlockSpec((1, lanes), lambda i: (0, 0))],
        out_specs=[pl.BlockSpec((tm, lanes), lambda i: (i, 0))] * 2,
        out_shape=[jax.ShapeDtypeStruct((m, lanes), F32),
                   jax.ShapeDtypeStruct((m, lanes), jnp.int32)],
        compiler_params=_params("parallel"),
        name="router",
    )(x, ln.reshape(1, d), w_pad, b_pad)


def dispatch_plan(sel, n_experts, tm):
    m = sel.shape[0]
    flat_e = sel[:, :TOP_K].reshape(-1)
    onehot = (flat_e[:, None] == jnp.arange(n_experts, dtype=jnp.int32)[None, :]).astype(jnp.int32)
    csum = jnp.cumsum(onehot, axis=0)
    rank = jnp.take_along_axis(csum, flat_e[:, None], axis=1)[:, 0] - 1
    counts = csum[-1]
    padded = ((counts + tm - 1) // tm) * tm
    ends = jnp.cumsum(padded)
    starts = ends - padded
    dest = (starts[flat_e] + rank).astype(jnp.int32)
    r = TOP_K * m + n_experts * tm
    src_token = jnp.zeros((r,), jnp.int32).at[dest].set(jnp.arange(TOP_K * m, dtype=jnp.int32) // TOP_K)
    tile_start = jnp.arange(r // tm, dtype=jnp.int32) * tm
    probe = jnp.minimum(tile_start, ends[-1] - 1)
    tile_expert = jnp.minimum(jnp.sum((ends[None, :] <= probe[:, None]).astype(jnp.int32), axis=1),
                              n_experts - 1).astype(jnp.int32)
    real_end = (starts + counts)[tile_expert]
    tile_rows = jnp.where(tile_start < ends[-1], jnp.clip(real_end - tile_start, 0, tm), 0)
    return src_token, dest, tile_expert, tile_rows.astype(jnp.int32)


def _combine_kernel(pos_ref, x_ref, w_ref, ys_hbm, gain_ref, o_ref, buf, sems, *, final_norm):
    i = pl.program_id(0)
    tc = x_ref.shape[0]

    def start(step, slot):
        def body(r, carry):
            t = step * tc + r
            for k in range(TOP_K):
                _row_copy(ys_hbm, buf.at[slot, k], pos_ref[TOP_K * t + k], r, sems.at[slot]).start()
            return carry
        lax.fori_loop(0, tc, body, 0, unroll=4)

    def wait(slot):
        def body(r, carry):
            for k in range(TOP_K):
                _row_copy(ys_hbm, buf.at[slot, k], 0, 0, sems.at[slot]).wait()
            return carry
        lax.fori_loop(0, tc, body, 0, unroll=4)

    slot = i % 2

    @pl.when(i == 0)
    def _():
        start(0, 0)

    @pl.when(i + 1 < pl.num_programs(0))
    def _():
        start(i + 1, 1 - slot)

    wait(slot)
    w = w_ref[...]
    out = x_ref[...] + (w[:, 0:1] * buf[slot, 0] + w[:, 1:2] * buf[slot, 1])
    if final_norm:
        out = _rms_scale(out) * gain_ref[...]
    o_ref[...] = out


def moe_combine(x, top_w, ys, dest, final_gain):
    m, d = x.shape
    tc = min(COMBINE_ROW_TILE, m)
    final_norm = final_gain is not None
    gain = (final_gain if final_norm else jnp.ones((d,), F32)).reshape(1, d)
    grid_spec = pltpu.PrefetchScalarGridSpec(
        num_scalar_prefetch=1,
        grid=(m // tc,),
        in_specs=[pl.BlockSpec((tc, d), lambda i, pos: (i, 0)),
                  pl.BlockSpec((tc, top_w.shape[1]), lambda i, pos: (i, 0)),
                  pl.BlockSpec(memory_space=pl.ANY),
                  pl.BlockSpec((1, d), lambda i, pos: (0, 0))],
        out_specs=pl.BlockSpec((tc, d), lambda i, pos: (i, 0)),
        scratch_shapes=[pltpu.VMEM((2, TOP_K, tc, d), F32), pltpu.SemaphoreType.DMA((2,))])
    return pl.pallas_call(
        functools.partial(_combine_kernel, final_norm=final_norm),
        grid_spec=grid_spec,
        out_shape=jax.ShapeDtypeStruct((m, d), F32),
        compiler_params=_params("arbitrary"),
        name="moe_combine",
    )(dest, x, top_w, ys, gain)


def moe_ffn(x, ln, router_w, router_b, wg, wu, wd, layer, final_gain):
    n_experts = router_w.shape[1]
    top_w, sel = router(x, ln, router_w, router_b)
    src_token, dest, tile_expert, tile_rows = dispatch_plan(sel, n_experts, EXPERT_ROW_TILE)
    ys = expert_ffn(x, src_token, tile_expert, tile_rows, ln, wg, wu, wd, layer)
    return moe_combine(x, top_w, ys, dest, final_gain)


def _final_norm_kernel(x_ref, g_ref, o_ref):
    o_ref[...] = _rms_scale(x_ref[...]) * g_ref[...]


def final_norm(x, gain):
    m, d = x.shape
    tm = min(ROW_TILE, m)
    return pl.pallas_call(
        _final_norm_kernel,
        grid=(m // tm,),
        in_specs=[pl.BlockSpec((tm, d), lambda i: (i, 0)), pl.BlockSpec((1, d), lambda i: (0, 0))],
        out_specs=pl.BlockSpec((tm, d), lambda i: (i, 0)),
        out_shape=jax.ShapeDtypeStruct((m, d), F32),
        compiler_params=_params("parallel"),
        name="final_norm",
    )(x, gain.reshape(1, d))


def kernel(x, ln_mix, ln_ffn, w_in, w_out, s5_lam_re, s5_lam_im, s5_log_dt, s5_b_re, s5_b_im,
           s5_c_re, s5_c_im, s5_d, s5_w_glu, s5_gain, ret_gain, ffn_w_gate, ffn_w_up, ffn_w_down,
           router_w, router_b, moe_w_gate, moe_w_up, moe_w_down, final_gain):
    batch, seq, d_model = x.shape
    depth = ln_mix.shape[0]
    d_s5, d_ret = s5_gain.shape[1], ret_gain.shape[1]
    groups = s5_lam_re.shape[1]
    t_len = min(S5_CHUNK, seq)
    n_chunks = seq // t_len
    tokens = batch * seq

    cos_t, sin_t = rope_tables(seq)
    s5_params = s5_layout_params(s5_lam_re, s5_lam_im, s5_log_dt, s5_b_re, s5_b_im,
                                 s5_c_re, s5_c_im, s5_d, s5_w_glu, t_len)
    w_in_b, w_out_b = w_in.astype(BF16), w_out.astype(BF16)
    ffn_wg_b, ffn_wu_b, ffn_wd_b = (ffn_w_gate.astype(BF16), ffn_w_up.astype(BF16),
                                    ffn_w_down.astype(BF16))
    moe_wg, moe_wu, moe_wd = moe_w_gate.astype(F32), moe_w_up.astype(F32), moe_w_down.astype(F32)
    xf = x.reshape(tokens, d_model).astype(F32)
    for i in range(depth):
        proj = norm_matmul(xf, ln_mix[i], w_in_b, i, BF16)
        u_t = s5_relayout_in(proj, d_s5, t_len)
        z = s5_groups(u_t, s5_params, i, batch)
        y_s5 = s5_relayout_out(z, t_len)
        y_ret = retention(proj, cos_t, sin_t, ret_gain[i], batch, seq, d_ret, d_s5 // d_ret)
        xf = out_proj(y_s5, y_ret, s5_gain[i], w_out_b, i, xf)
        j = i // 2
        last = i == depth - 1
        if i % 2 == 0:
            xf = dense_ffn(xf, ln_ffn[i], ffn_wg_b, ffn_wu_b, ffn_wd_b, j)
            if last:
                xf = final_norm(xf, final_gain)
        else:
            xf = moe_ffn(xf, ln_ffn[i], router_w[j], router_b[j], moe_wg, moe_wu, moe_wd, j,
                         final_gain if last else None)
    return xf.reshape(batch, seq, d_model).astype(x.dtype)
```
